```python
import math
import jax, jax.numpy as jnp
from jax import lax
import numpy as np

D_MODEL = 1024
BATCH = 2
SEQ = 8192
DEPTH = 2

GRID_W = 64
CTX_LEN = 256
MLA_HEADS = 8
QK_NOPE = 64
QK_ROPE = 32
V_HEAD = 64
Q_LORA = 384
KV_LORA = 256
CONV_CH = 256
CONV_WIDTH = 31
SC_CH = 256
SC_WIDTH = 3
N_BRANCH = 3
D_FF = int(math.ceil(8 * D_MODEL / 3 / 256)) * 256
N_MOD = 6
ROPE_THETA = 10000.0
EPS = 1e-6
Q_BLOCK = 128

Q_A_END = Q_LORA
KV_A_END = Q_A_END + KV_LORA + QK_ROPE
GLU_END = KV_A_END + 2 * CONV_CH
SC_END = GLU_END + 3 * SC_CH
GATE_END = SC_END + N_BRANCH * D_MODEL
IN_COLS = GATE_END

kernel_name = 'hybrid_mla_conformer_shortconv_dit'


def rms_norm(x, g):
    xf = x.astype(jnp.float32)
    y = xf * lax.rsqrt(jnp.mean(xf * xf, axis=-1, keepdims=True) + EPS)
    return (y * g.astype(jnp.float32)).astype(x.dtype)


def layer_norm(x, g, b):
    xf = x.astype(jnp.float32)
    mu = jnp.mean(xf, axis=-1, keepdims=True)
    var = jnp.mean(jnp.square(xf - mu), axis=-1, keepdims=True)
    y = (xf - mu) * lax.rsqrt(var + EPS)
    return (y * g.astype(jnp.float32) + b.astype(jnp.float32)).astype(x.dtype)


def modulate(x, shift, scale):
    return x * (1 + scale) + shift


def depthwise_conv(u, w):
    k = w.shape[0]
    pad = (k - 1) // 2
    return lax.conv_general_dilated(
        u, w[:, None, :].astype(u.dtype), window_strides=(1,), padding=[(pad, pad)],
        dimension_numbers=('NWC', 'WIO', 'NWC'), feature_group_count=u.shape[-1])


def axial_rope_tables(n):
    rows = n // GRID_W
    row = jnp.repeat(jnp.arange(rows, dtype=jnp.float32), GRID_W)
    col = jnp.tile(jnp.arange(GRID_W, dtype=jnp.float32), rows)
    half = QK_ROPE // 2
    inv = jnp.power(jnp.float32(ROPE_THETA), -jnp.arange(0, half, 2, dtype=jnp.float32) / half)
    ang_r = row[:, None] * inv
    ang_c = col[:, None] * inv
    return (jnp.cos(ang_r), jnp.sin(ang_r), jnp.cos(ang_c), jnp.sin(ang_c))


def _rotate(x, cos, sin):
    x1, x2 = jnp.split(x, 2, axis=-1)
    return jnp.concatenate([x1 * cos - x2 * sin, x1 * sin + x2 * cos], axis=-1)


def apply_axial_rope(x, tables):
    extra = x.ndim - 3
    cr, sr, cc, sc = [t.reshape(t.shape[0], *([1] * extra), t.shape[1]).astype(x.dtype) for t in tables]
    xr, xc = jnp.split(x, 2, axis=-1)
    return jnp.concatenate([_rotate(xr, cr, sr), _rotate(xc, cc, sc)], axis=-1)


def mla_queries(q_a, q_norm_g, w_q_b):
    q = rms_norm(q_a, q_norm_g) @ w_q_b
    q = q.reshape(*q_a.shape[:-1], MLA_HEADS, QK_NOPE + QK_ROPE)
    return q[..., :QK_NOPE], q[..., QK_NOPE:]


def mla_keys_values(kv_a, kv_norm_g, w_kv_b):
    c_kv = rms_norm(kv_a[..., :KV_LORA], kv_norm_g)
    k_rope = kv_a[..., KV_LORA:]
    kv = (c_kv @ w_kv_b).reshape(*kv_a.shape[:-1], MLA_HEADS, QK_NOPE + V_HEAD)
    return kv[..., :QK_NOPE], k_rope, kv[..., QK_NOPE:]


def attend(q_nope, q_rope, k_nope, k_rope, v):
    scale = (QK_NOPE + QK_ROPE) ** -0.5
    s = jnp.einsum('bqhd,bkhd->bhqk', q_nope, k_nope) + jnp.einsum('bqhr,bkr->bhqk', q_rope, k_rope)
    p = jax.nn.softmax(s.astype(jnp.float32) * scale, axis=-1).astype(v.dtype)
    return jnp.einsum('bhqk,bkhd->bqhd', p, v)


def blockwise_attend(q_nope, q_rope, k_nope, k_rope, v):
    b, n = q_nope.shape[:2]
    nb = n // Q_BLOCK

    def to_blocks(t):
        return jnp.moveaxis(t.reshape(b, nb, Q_BLOCK, *t.shape[2:]), 1, 0)

    out = lax.map(lambda qs: attend(qs[0], qs[1], k_nope, k_rope, v),
                  (to_blocks(q_nope), to_blocks(q_rope)))
    return jnp.moveaxis(out, 0, 1).reshape(b, n, MLA_HEADS * V_HEAD)


def conformer_conv(glu_in, dw, dw_b, ln_g, ln_b, w_out):
    a, g = jnp.split(glu_in, 2, axis=-1)
    u = depthwise_conv(a * jax.nn.sigmoid(g), dw) + dw_b
    u = jax.nn.silu(layer_norm(u, ln_g, ln_b))
    return u @ w_out


def short_conv(sc_in, dw, w_out):
    bg, cg, xin = jnp.split(sc_in, 3, axis=-1)
    return (bg * depthwise_conv(cg * xin, dw)) @ w_out


def merge_branches(gate_in, y_att, y_conv, y_sc, w_o):
    g_att, g_conv, g_sc = jnp.split(jax.nn.sigmoid(gate_in), N_BRANCH, axis=-1)
    return (g_att * y_att + g_conv * y_conv + g_sc * y_sc) @ w_o


def swiglu(h, w1, w3, w2):
    return (jax.nn.silu(h @ w1) * (h @ w3)) @ w2


def setup_inputs(seed: int = 0) -> dict:
    key = jax.random.key(seed)
    ks = iter(jax.random.split(key, 32))

    def nrm(shape, scale):
        return jax.random.normal(next(ks), shape, jnp.float32) * scale

    def gain(shape):
        return 1.0 + nrm(shape, 0.02)

    return {
        'x': nrm((BATCH, SEQ, D_MODEL), 1.0),
        'c': nrm((BATCH, D_MODEL), 1.0),
        'ctx': nrm((BATCH, CTX_LEN, D_MODEL), 1.0),
        'c_ctx': nrm((D_MODEL,), 1.0),
        'w_mod': nrm((DEPTH, D_MODEL, N_MOD * D_MODEL), 0.5 * D_MODEL ** -0.5),
        'b_mod': nrm((DEPTH, N_MOD * D_MODEL), 0.01),
        'ln1_g': gain((DEPTH, D_MODEL)),
        'w_in': nrm((DEPTH, D_MODEL, IN_COLS), D_MODEL ** -0.5),
        'q_a_norm_g': gain((DEPTH, Q_LORA)),
        'w_q_b': nrm((DEPTH, Q_LORA, MLA_HEADS * (QK_NOPE + QK_ROPE)), Q_LORA ** -0.5),
        'kv_a_norm_g': gain((DEPTH, KV_LORA)),
        'w_kv_b': nrm((DEPTH, KV_LORA, MLA_HEADS * (QK_NOPE + V_HEAD)), KV_LORA ** -0.5),
        'w_mla_o': nrm((DEPTH, MLA_HEADS * V_HEAD, D_MODEL), (MLA_HEADS * V_HEAD) ** -0.5),
        'conv_dw': nrm((DEPTH, CONV_WIDTH, CONV_CH), CONV_WIDTH ** -0.5),
        'conv_dw_b': nrm((DEPTH, CONV_CH), 0.01),
        'conv_ln_g': gain((DEPTH, CONV_CH)),
        'conv_ln_b': nrm((DEPTH, CONV_CH), 0.01),
        'w_conv_o': nrm((DEPTH, CONV_CH, D_MODEL), CONV_CH ** -0.5),
        'sc_dw': nrm((DEPTH, SC_WIDTH, SC_CH), SC_WIDTH ** -0.5),
        'w_sc_o': nrm((DEPTH, SC_CH, D_MODEL), SC_CH ** -0.5),
        'w_o': nrm((DEPTH, D_MODEL, D_MODEL), D_MODEL ** -0.5),
        'ln2_g': gain((DEPTH, D_MODEL)),
        'w_ff1': nrm((DEPTH, D_MODEL, D_FF), D_MODEL ** -0.5),
        'w_ff3': nrm((DEPTH, D_MODEL, D_FF), D_MODEL ** -0.5),
        'w_ff2': nrm((DEPTH, D_FF, D_MODEL), D_FF ** -0.5),
        'final_g': gain((D_MODEL,)),
    }


def reference(x, c, ctx, c_ctx, w_mod, b_mod, ln1_g, w_in, q_a_norm_g, w_q_b, kv_a_norm_g,
              w_kv_b, w_mla_o, conv_dw, conv_dw_b, conv_ln_g, conv_ln_b, w_conv_o, sc_dw, w_sc_o,
              w_o, ln2_g, w_ff1, w_ff3, w_ff2, final_g):
    b, n, _ = x.shape
    n_ctx = ctx.shape[1]
    rope = axial_rope_tables(n)
    c_act = jax.nn.silu(c)
    cc_act = jax.nn.silu(c_ctx)
    xc = ctx
    for l in range(DEPTH):
        last = l == DEPTH - 1
        sh1, sc1, g1, sh2, sc2, g2 = jnp.split((c_act @ w_mod[l] + b_mod[l])[:, None, :], N_MOD, axis=-1)
        mod_c = jnp.split(cc_act @ w_mod[l] + b_mod[l], N_MOD, axis=-1)

        hc = modulate(rms_norm(xc, ln1_g[l]), mod_c[0], mod_c[1])
        if last:
            kvc_a = hc @ w_in[l][:, Q_A_END:KV_A_END]
        else:
            projc = hc @ w_in[l]
            kvc_a = projc[..., Q_A_END:KV_A_END]
        kc_nope, kc_rope, vc = mla_keys_values(kvc_a, kv_a_norm_g[l], w_kv_b[l])
        if not last:
            qc_nope, qc_rope = mla_queries(projc[..., :Q_A_END], q_a_norm_g[l], w_q_b[l])
            att_c = attend(qc_nope, qc_rope, kc_nope, kc_rope, vc).reshape(b, n_ctx, MLA_HEADS * V_HEAD) @ w_mla_o[l]
            conv_c = conformer_conv(projc[..., KV_A_END:GLU_END], conv_dw[l], conv_dw_b[l],
                                    conv_ln_g[l], conv_ln_b[l], w_conv_o[l])
            scv_c = short_conv(projc[..., GLU_END:SC_END], sc_dw[l], w_sc_o[l])
            xc = xc + mod_c[2] * merge_branches(projc[..., SC_END:GATE_END], att_c, conv_c, scv_c, w_o[l])
            hc2 = modulate(rms_norm(xc, ln2_g[l]), mod_c[3], mod_c[4])
            xc = xc + mod_c[5] * swiglu(hc2, w_ff1[l], w_ff3[l], w_ff2[l])

        h = modulate(rms_norm(x, ln1_g[l]), sh1, sc1)
        proj = h @ w_in[l]
        q_nope, q_rope = mla_queries(proj[..., :Q_A_END], q_a_norm_g[l], w_q_b[l])
        k_nope, k_rope, v = mla_keys_values(proj[..., Q_A_END:KV_A_END], kv_a_norm_g[l], w_kv_b[l])
        q_rope = apply_axial_rope(q_rope, rope)
        k_rope = apply_axial_rope(k_rope, rope)
        att = blockwise_attend(q_nope, q_rope,
                               jnp.concatenate([k_nope, kc_nope], axis=1),
                               jnp.concatenate([k_rope, kc_rope], axis=1),
                               jnp.concatenate([v, vc], axis=1)) @ w_mla_o[l]
        y_conv = conformer_conv(proj[..., KV_A_END:GLU_END], conv_dw[l], conv_dw_b[l],
                                conv_ln_g[l], conv_ln_b[l], w_conv_o[l])
        y_sc = short_conv(proj[..., GLU_END:SC_END], sc_dw[l], w_sc_o[l])
        x = x + g1 * merge_branches(proj[..., SC_END:GATE_END], att, y_conv, y_sc, w_o[l])
        h2 = modulate(rms_norm(x, ln2_g[l]), sh2, sc2)
        x = x + g2 * swiglu(h2, w_ff1[l], w_ff3[l], w_ff2[l])
    return rms_norm(x, final_g)
```

```python
import functools
import math

import jax
import jax.numpy as jnp
from jax import lax
from jax.experimental import pallas as pl
from jax.experimental.pallas import tpu as pltpu

F32 = jnp.float32
BF16 = jnp.bfloat16

GRID_W = 64
MLA_HEADS = 8
QK_NOPE = 64
QK_ROPE = 32
V_HEAD = 64
Q_LORA = 384
KV_LORA = 256
CONV_CH = 256
CONV_WIDTH = 31
SC_CH = 256
SC_WIDTH = 3
N_MOD = 6
ROPE_THETA = 10000.0
EPS = 1e-6

LANES = 128
HEAD_PAD = LANES
TM = 256
HALO = 16
TK = 1024
MOD_ROWS = 8
MOD_TN = 1536
VMEM_LIMIT = 56 * 1024 * 1024

Q_SCALE = (QK_NOPE + QK_ROPE) ** -0.5 * math.log2(math.e)

A_Q = 0
A_KV = A_Q + Q_LORA
A_GLU_A = A_KV + KV_LORA
A_GLU_G = A_GLU_A + CONV_CH
A_SC_B = A_GLU_G + CONV_CH
A_SC_C = A_SC_B + SC_CH
A_SC_X = A_SC_C + SC_CH
A_KR = A_SC_X + SC_CH
A_KR_SWAP = A_KR + HEAD_PAD
A_COLS = A_KR_SWAP + HEAD_PAD


def _dot(a, b):
    return jnp.dot(a, b, preferred_element_type=F32)


def _rms(x, g):
    return x * lax.rsqrt(jnp.mean(x * x, axis=-1, keepdims=True) + EPS) * g


def _norm_modulate(x, g, shift, scale):
    return _rms(x, g) * (1.0 + scale) + shift


def _mod_kernel(act_ref, w_ref, b_ref, o_ref):
    a = act_ref[...]
    a = a * jax.nn.sigmoid(a)
    o_ref[0] = _dot(a.astype(BF16), w_ref[0].astype(BF16)) + b_ref[0]


def _modulation(act, w_mod, b_mod):
    depth, d, n = w_mod.shape
    return pl.pallas_call(
        _mod_kernel,
        out_shape=jax.ShapeDtypeStruct((depth, MOD_ROWS, n), F32),
        grid=(depth, n // MOD_TN),
        in_specs=[
            pl.BlockSpec((MOD_ROWS, d), lambda l, j: (0, 0)),
            pl.BlockSpec((1, d, MOD_TN), lambda l, j: (l, 0, j)),
            pl.BlockSpec((1, 1, MOD_TN), lambda l, j: (l, 0, j)),
        ],
        out_specs=pl.BlockSpec((1, MOD_ROWS, MOD_TN), lambda l, j: (l, 0, j)),
        compiler_params=pltpu.CompilerParams(
            dimension_semantics=("arbitrary", "arbitrary"), vmem_limit_bytes=VMEM_LIMIT),
        name="modulation",
    )(act, w_mod, b_mod.reshape(depth, 1, n))


def _inproj_kernel(x_ref, mod_ref, lng_ref, wa_ref, qg_ref, wqm_ref, wqs_ref, kvg_ref, wkt_ref,
                   wv_ref, rc_ref, rs_ref, q_ref, kt_ref, v_ref, glu_ref, cx_ref, bg_ref):
    x = x_ref[0]
    h = _norm_modulate(x, lng_ref[...], mod_ref[0, 0, 0:1, :], mod_ref[0, 0, 1:2, :])
    proj = _dot(h.astype(BF16), wa_ref[...])
    rc = rc_ref[...]
    rs = rs_ref[...]

    qn = _rms(proj[:, A_Q:A_Q + Q_LORA], qg_ref[...] * Q_SCALE).astype(BF16)
    q = (_dot(qn, wqm_ref[...]) * jnp.concatenate([rc] * MLA_HEADS, axis=1)
         + _dot(qn, wqs_ref[...]) * jnp.concatenate([rs] * MLA_HEADS, axis=1))
    q_ref[0] = q.astype(BF16)

    cn = _rms(proj[:, A_KV:A_KV + KV_LORA], kvg_ref[...]).astype(BF16)
    v_ref[0] = _dot(cn, wv_ref[...]).astype(BF16)
    kr = proj[:, A_KR:A_KR + HEAD_PAD] * rc + proj[:, A_KR_SWAP:A_KR_SWAP + HEAD_PAD] * rs
    kt = lax.dot_general(wkt_ref[...], cn, (((1,), (1,)), ((), ())),
                         preferred_element_type=F32)
    kt = kt + jnp.concatenate([kr.T] * MLA_HEADS, axis=0)
    kt_ref[0] = kt.astype(BF16)

    glu_ref[0] = proj[:, A_GLU_A:A_GLU_A + CONV_CH] * jax.nn.sigmoid(proj[:, A_GLU_G:A_GLU_G + CONV_CH])
    bg_ref[0] = proj[:, A_SC_B:A_SC_B + SC_CH]
    cx_ref[0] = proj[:, A_SC_C:A_SC_C + SC_CH] * proj[:, A_SC_X:A_SC_X + SC_CH]


def _inproj(xa, mods, lng, wa, qg, wqm, wqs, kvg, wkt, wv, rc, rs, n_lat_tiles):
    b, t_all, d = xa.shape
    nt = t_all // TM
    hq = MLA_HEADS * HEAD_PAD
    const = lambda bi, i: (0, 0)
    row = lambda bi, i: (bi, i, 0)
    return pl.pallas_call(
        _inproj_kernel,
        out_shape=(
            jax.ShapeDtypeStruct((b, t_all, hq), BF16),
            jax.ShapeDtypeStruct((b, hq, t_all), BF16),
            jax.ShapeDtypeStruct((b, t_all, MLA_HEADS * V_HEAD), BF16),
            jax.ShapeDtypeStruct((b, t_all, CONV_CH), F32),
            jax.ShapeDtypeStruct((b, t_all, SC_CH), F32),
            jax.ShapeDtypeStruct((b, t_all, SC_CH), F32),
        ),
        grid=(b, nt),
        in_specs=[
            pl.BlockSpec((1, TM, d), row),
            pl.BlockSpec((1, 1, MOD_ROWS, d), lambda bi, i: (bi, i // n_lat_tiles, 0, 0)),
            pl.BlockSpec((1, d), const),
            pl.BlockSpec((d, A_COLS), const),
            pl.BlockSpec((1, Q_LORA), const),
            pl.BlockSpec((Q_LORA, hq), const),
            pl.BlockSpec((Q_LORA, hq), const),
            pl.BlockSpec((1, KV_LORA), const),
            pl.BlockSpec((hq, KV_LORA), const),
            pl.BlockSpec((KV_LORA, MLA_HEADS * V_HEAD), const),
            pl.BlockSpec((TM, HEAD_PAD), lambda bi, i: (i, 0)),
            pl.BlockSpec((TM, HEAD_PAD), lambda bi, i: (i, 0)),
        ],
        out_specs=(
            pl.BlockSpec((1, TM, hq), row),
            pl.BlockSpec((1, hq, TM), lambda bi, i: (bi, 0, i)),
            pl.BlockSpec((1, TM, MLA_HEADS * V_HEAD), row),
            pl.BlockSpec((1, TM, CONV_CH), row),
            pl.BlockSpec((1, TM, SC_CH), row),
            pl.BlockSpec((1, TM, SC_CH), row),
        ),
        compiler_params=pltpu.CompilerParams(
            dimension_semantics=("arbitrary", "arbitrary"), vmem_limit_bytes=VMEM_LIMIT),
        name="inproj",
    )(xa, mods, lng, wa, qg, wqm, wqs, kvg, wkt, wv, rc, rs)


def _attn_kernel(q_ref, kt_ref, v_ref, o_ref, *, n_lat, n_lat_tiles):
    i = pl.program_id(2)
    n_steps = jnp.where(i < n_lat_tiles, n_lat // TK, 0)
    t_all = kt_ref.shape[2]
    outs = []
    for hh in range(2):
        q = q_ref[0, :, hh * HEAD_PAD:(hh + 1) * HEAD_PAD]
        s = _dot(q, kt_ref[0, hh * HEAD_PAD:(hh + 1) * HEAD_PAD, n_lat:t_all])
        m = jnp.max(s, axis=-1, keepdims=True)
        p = jnp.exp2(s - m)
        l = jnp.sum(p, axis=-1, keepdims=True)
        acc = _dot(p.astype(BF16), v_ref[0, n_lat:t_all, :])

        def step(kb, carry, hh=hh, q=q):
            m, l, acc = carry
            start = pl.multiple_of(kb * TK, TK)
            s = _dot(q, kt_ref[0, hh * HEAD_PAD:(hh + 1) * HEAD_PAD, pl.ds(start, TK)])
            m_new = jnp.maximum(m, jnp.max(s, axis=-1, keepdims=True))
            alpha = jnp.exp2(m - m_new)
            p = jnp.exp2(s - m_new)
            l = alpha * l + jnp.sum(p, axis=-1, keepdims=True)
            acc = alpha * acc + _dot(p.astype(BF16), v_ref[0, pl.ds(start, TK), :])
            return m_new, l, acc

        m, l, acc = lax.fori_loop(0, n_steps, step, (m, l, acc))
        outs.append(acc / l)
    lane = lax.broadcasted_iota(jnp.int32, outs[0].shape, 1)
    o_ref[0] = jnp.where(lane < V_HEAD, outs[0], outs[1]).astype(BF16)


def _attention(q, kt, v, n_q_tiles, n_lat):
    b, t_all, _ = q.shape
    pairs = MLA_HEADS // 2
    return pl.pallas_call(
        functools.partial(_attn_kernel, n_lat=n_lat, n_lat_tiles=n_lat // TM),
        out_shape=jax.ShapeDtypeStruct((b, n_q_tiles * TM, MLA_HEADS * V_HEAD), BF16),
        grid=(b, pairs, n_q_tiles),
        in_specs=[
            pl.BlockSpec((1, TM, 2 * HEAD_PAD), lambda bi, j, i: (bi, i, j)),
            pl.BlockSpec((1, 2 * HEAD_PAD, t_all), lambda bi, j, i: (bi, j, 0)),
            pl.BlockSpec((1, t_all, 2 * V_HEAD), lambda bi, j, i: (bi, 0, j)),
        ],
        out_specs=pl.BlockSpec((1, TM, 2 * V_HEAD), lambda bi, j, i: (bi, i, j)),
        compiler_params=pltpu.CompilerParams(
            dimension_semantics=("arbitrary", "arbitrary", "arbitrary"), vmem_limit_bytes=VMEM_LIMIT),
        name="attention",
    )(q, kt, v)


def _fill_halo_buffer(buf, cur_ref, prev_ref, next_ref, keep_prev, keep_next):
    buf[0:HALO, :] = prev_ref[0] * keep_prev
    buf[HALO:HALO + TM, :] = cur_ref[0]
    buf[HALO + TM:HALO + TM + HALO, :] = next_ref[0] * keep_next


def _depthwise(buf, w_ref, width):
    base = HALO - (width - 1) // 2
    acc = w_ref[0:1, :] * buf[base:base + TM, :]
    for k in range(1, width):
        acc = acc + w_ref[k:k + 1, :] * buf[base + k:base + k + TM, :]
    return acc


def _merge_kernel(x_ref, mod_ref, lng_ref, att_ref, glu_ref, glu_p_ref, glu_n_ref, cx_ref, cx_p_ref,
                  cx_n_ref, bg_ref, wg_ref, wmo_ref, dw_ref, dwb_ref, cg_ref, cb_ref, wco_ref,
                  scw_ref, wso_ref, wo_ref, o_ref, cbuf, sbuf, *, n_lat_tiles):
    i = pl.program_id(1)
    is_ctx = i >= n_lat_tiles
    keep_prev = jnp.where((i == 0) | is_ctx, 0.0, 1.0)
    keep_next = jnp.where((i == n_lat_tiles - 1) | is_ctx, 0.0, 1.0)
    _fill_halo_buffer(cbuf, glu_ref, glu_p_ref, glu_n_ref, keep_prev, keep_next)
    _fill_halo_buffer(sbuf, cx_ref, cx_p_ref, cx_n_ref, keep_prev, keep_next)

    u = _depthwise(cbuf, dw_ref, CONV_WIDTH) + dwb_ref[...]
    mu = jnp.mean(u, axis=-1, keepdims=True)
    uc = u - mu
    var = jnp.mean(uc * uc, axis=-1, keepdims=True)
    u = uc * lax.rsqrt(var + EPS) * cg_ref[...] + cb_ref[...]
    u = u * jax.nn.sigmoid(u)
    y_conv = _dot(u.astype(BF16), wco_ref[...])

    sc = bg_ref[0] * _depthwise(sbuf, scw_ref, SC_WIDTH)
    y_sc = _dot(sc.astype(BF16), wso_ref[...])

    y_att = _dot(att_ref[0], wmo_ref[...])

    x = x_ref[0]
    d = x.shape[-1]
    h = _norm_modulate(x, lng_ref[...], mod_ref[0, 0, 0:1, :], mod_ref[0, 0, 1:2, :])
    gates = jax.nn.sigmoid(_dot(h.astype(BF16), wg_ref[...]))
    merged = gates[:, 0:d] * y_att + gates[:, d:2 * d] * y_conv + gates[:, 2 * d:3 * d] * y_sc
    o_ref[0] = x + mod_ref[0, 0, 2:3, :] * _dot(merged.astype(BF16), wo_ref[...])


def _merge(xa, mods, lng, att, glu, cx, bg, wg, wmo, dw, dwb, cg, cb, wco, scw, wso, wo, n_tiles,
           n_lat_tiles):
    b, t_all, d = xa.shape
    per = TM // HALO
    last_halo = t_all // HALO - 1
    const = lambda bi, i: (0, 0)
    row = lambda bi, i: (bi, i, 0)
    prev = lambda bi, i: (bi, jnp.maximum(i * per - 1, 0), 0)
    nxt = lambda bi, i: (bi, jnp.minimum((i + 1) * per, last_halo), 0)
    hv = MLA_HEADS * V_HEAD
    return pl.pallas_call(
        functools.partial(_merge_kernel, n_lat_tiles=n_lat_tiles),
        out_shape=jax.ShapeDtypeStruct((b, n_tiles * TM, d), F32),
        grid=(b, n_tiles),
        in_specs=[
            pl.BlockSpec((1, TM, d), row),
            pl.BlockSpec((1, 1, MOD_ROWS, d), lambda bi, i: (bi, i // n_lat_tiles, 0, 0)),
            pl.BlockSpec((1, d), const),
            pl.BlockSpec((1, TM, hv), row),
            pl.BlockSpec((1, TM, CONV_CH), row),
            pl.BlockSpec((1, HALO, CONV_CH), prev),
            pl.BlockSpec((1, HALO, CONV_CH), nxt),
            pl.BlockSpec((1, TM, SC_CH), row),
            pl.BlockSpec((1, HALO, SC_CH), prev),
            pl.BlockSpec((1, HALO, SC_CH), nxt),
            pl.BlockSpec((1, TM, SC_CH), row),
            pl.BlockSpec((d, 3 * d), const),
            pl.BlockSpec((hv, d), const),
            pl.BlockSpec((CONV_WIDTH, CONV_CH), const),
            pl.BlockSpec((1, CONV_CH), const),
            pl.BlockSpec((1, CONV_CH), const),
            pl.BlockSpec((1, CONV_CH), const),
            pl.BlockSpec((CONV_CH, d), const),
            pl.BlockSpec((SC_WIDTH, SC_CH), const),
            pl.BlockSpec((SC_CH, d), const),
            pl.BlockSpec((d, d), const),
        ],
        out_specs=pl.BlockSpec((1, TM, d), row),
        scratch_shapes=[
            pltpu.VMEM((TM + 2 * HALO, CONV_CH), F32),
            pltpu.VMEM((TM + 2 * HALO, SC_CH), F32),
        ],
        compiler_params=pltpu.CompilerParams(
            dimension_semantics=("arbitrary", "arbitrary"), vmem_limit_bytes=VMEM_LIMIT),
        name="merge",
    )(xa, mods, lng, att, glu, glu, glu, cx, cx, cx, bg, wg, wmo, dw, dwb, cg, cb, wco, scw, wso, wo)


def _ffn_kernel(x_ref, mod_ref, lng_ref, w1_ref, w3_ref, w2_ref, fg_ref, o_ref, *, final_norm):
    x = x_ref[0]
    h = _norm_modulate(x, lng_ref[...], mod_ref[0, 0, 3:4, :], mod_ref[0, 0, 4:5, :]).astype(BF16)
    a = _dot(h, w1_ref[...])
    a = a * jax.nn.sigmoid(a) * _dot(h, w3_ref[...])
    y = x + mod_ref[0, 0, 5:6, :] * _dot(a.astype(BF16), w2_ref[...])
    if final_norm:
        y = _rms(y, fg_ref[...])
    o_ref[0] = y


def _ffn(xa, mods, lng, w1, w3, w2, fg, n_tiles, n_lat_tiles, final_norm):
    b, t_all, d = xa.shape
    dff = w1.shape[1]
    const = lambda bi, i: (0, 0)
    row = lambda bi, i: (bi, i, 0)
    return pl.pallas_call(
        functools.partial(_ffn_kernel, final_norm=final_norm),
        out_shape=jax.ShapeDtypeStruct((b, n_tiles * TM, d), F32),
        grid=(b, n_tiles),
        in_specs=[
            pl.BlockSpec((1, TM, d), row),
            pl.BlockSpec((1, 1, MOD_ROWS, d), lambda bi, i: (bi, i // n_lat_tiles, 0, 0)),
            pl.BlockSpec((1, d), const),
            pl.BlockSpec((d, dff), const),
            pl.BlockSpec((d, dff), const),
            pl.BlockSpec((dff, d), const),
            pl.BlockSpec((1, d), const),
        ],
        out_specs=pl.BlockSpec((1, TM, d), row),
        compiler_params=pltpu.CompilerParams(
            dimension_semantics=("arbitrary", "arbitrary"), vmem_limit_bytes=VMEM_LIMIT),
        name="ffn",
    )(xa, mods, lng, w1, w3, w2, fg)


def _rope_patterns(n_lat, n_ctx):
    t = jnp.arange(n_lat)
    row = (t // GRID_W).astype(F32)
    col = (t % GRID_W).astype(F32)
    half = QK_ROPE // 2
    inv = jnp.power(jnp.float32(ROPE_THETA), -jnp.arange(0, half, 2, dtype=F32) / half)
    ang_r = row[:, None] * inv
    ang_c = col[:, None] * inv
    cos = jnp.concatenate([jnp.cos(ang_r)] * 2 + [jnp.cos(ang_c)] * 2, axis=1)
    sin = jnp.concatenate([-jnp.sin(ang_r), jnp.sin(ang_r), -jnp.sin(ang_c), jnp.sin(ang_c)], axis=1)
    pad = HEAD_PAD - QK_NOPE - QK_ROPE
    rc = jnp.concatenate([jnp.ones((n_lat, QK_NOPE), F32), cos, jnp.zeros((n_lat, pad), F32)], axis=1)
    rs = jnp.concatenate([jnp.zeros((n_lat, QK_NOPE), F32), sin, jnp.zeros((n_lat, pad), F32)], axis=1)
    rc_ctx = jnp.concatenate([jnp.ones((n_ctx, QK_NOPE + QK_ROPE), F32), jnp.zeros((n_ctx, pad), F32)], axis=1)
    return (jnp.concatenate([rc, rc_ctx], axis=0),
            jnp.concatenate([rs, jnp.zeros((n_ctx, HEAD_PAD), F32)], axis=0))


def _swap_rope_columns(w):
    idx = jnp.arange(QK_ROPE) ^ (QK_ROPE // 4)
    return w[..., idx]


def _layer_weights(w_in, w_q_b, w_kv_b):
    d = w_in.shape[0]
    q_end = Q_LORA
    kv_end = q_end + KV_LORA + QK_ROPE
    sc_end = kv_end + 2 * CONV_CH + 3 * SC_CH
    w_kr = w_in[:, q_end + KV_LORA:kv_end]
    pad_lo = jnp.zeros((d, QK_NOPE), w_in.dtype)
    pad_hi = jnp.zeros((d, HEAD_PAD - QK_NOPE - QK_ROPE), w_in.dtype)
    wa = jnp.concatenate([
        w_in[:, :q_end + KV_LORA], w_in[:, kv_end:sc_end],
        pad_lo, w_kr, pad_hi, pad_lo, _swap_rope_columns(w_kr), pad_hi], axis=1).astype(BF16)
    wg = w_in[:, sc_end:].astype(BF16)

    wq = w_q_b.reshape(Q_LORA, MLA_HEADS, QK_NOPE + QK_ROPE)
    zq = jnp.zeros((Q_LORA, MLA_HEADS, HEAD_PAD - QK_NOPE - QK_ROPE), w_q_b.dtype)
    wqm = jnp.concatenate([wq, zq], axis=2).reshape(Q_LORA, MLA_HEADS * HEAD_PAD).astype(BF16)
    wqs = jnp.concatenate([jnp.zeros((Q_LORA, MLA_HEADS, QK_NOPE), w_q_b.dtype),
                           _swap_rope_columns(wq[:, :, QK_NOPE:]), zq], axis=2)
    wqs = wqs.reshape(Q_LORA, MLA_HEADS * HEAD_PAD).astype(BF16)

    wkv = w_kv_b.reshape(KV_LORA, MLA_HEADS, QK_NOPE + V_HEAD)
    wk = jnp.concatenate([wkv[:, :, :QK_NOPE],
                          jnp.zeros((KV_LORA, MLA_HEADS, HEAD_PAD - QK_NOPE), w_kv_b.dtype)], axis=2)
    wkt = wk.reshape(KV_LORA, MLA_HEADS * HEAD_PAD).T.astype(BF16)
    wv = wkv[:, :, QK_NOPE:].reshape(KV_LORA, MLA_HEADS * V_HEAD).astype(BF16)
    return wa, wg, wqm, wqs, wkt, wv


def kernel(x, c, ctx, c_ctx, w_mod, b_mod, ln1_g, w_in, q_a_norm_g, w_q_b, kv_a_norm_g, w_kv_b, w_mla_o, conv_dw, conv_dw_b, conv_ln_g, conv_ln_b, w_conv_o, sc_dw, w_sc_o, w_o, ln2_g, w_ff1, w_ff3, w_ff2, final_g):
    b, n_lat, d = x.shape
    n_ctx = ctx.shape[1]
    depth = w_mod.shape[0]
    assert n_lat % TK == 0 and n_ctx % TM == 0 and n_lat % TM == 0 and b + 1 <= MOD_ROWS
    n_lat_tiles = n_lat // TM
    n_all_tiles = (n_lat + n_ctx) // TM

    act = jnp.concatenate([c, c_ctx[None, :], jnp.zeros((MOD_ROWS - b - 1, d), F32)], axis=0)
    mod = _modulation(act, w_mod, b_mod).reshape(depth, MOD_ROWS, N_MOD, d)
    mod = jnp.pad(mod, ((0, 0), (0, 0), (0, MOD_ROWS - N_MOD), (0, 0)))
    mod_ctx = jnp.broadcast_to(mod[:, b][:, None], (depth, b, MOD_ROWS, d))
    mods = jnp.stack([mod[:, :b], mod_ctx], axis=2)

    rc, rs = _rope_patterns(n_lat, n_ctx)
    xa = jnp.concatenate([x, ctx], axis=1)
    row = lambda v: v.reshape(1, -1)

    for l in range(depth):
        last = l == depth - 1
        wa, wg, wqm, wqs, wkt, wv = _layer_weights(w_in[l], w_q_b[l], w_kv_b[l])
        q, kt, v, glu, cx, bg = _inproj(
            xa, mods[l], row(ln1_g[l]), wa, row(q_a_norm_g[l]), wqm, wqs, row(kv_a_norm_g[l]),
            wkt, wv, rc, rs, n_lat_tiles)
        n_tiles = n_lat_tiles if last else n_all_tiles
        att = _attention(q, kt, v, n_tiles, n_lat)
        xa = _merge(xa, mods[l], row(ln1_g[l]), att, glu, cx, bg, wg, w_mla_o[l].astype(BF16),
                    conv_dw[l], row(conv_dw_b[l]), row(conv_ln_g[l]), row(conv_ln_b[l]),
                    w_conv_o[l].astype(BF16), sc_dw[l], w_sc_o[l].astype(BF16), w_o[l].astype(BF16),
                    n_tiles, n_lat_tiles)
        xa = _ffn(xa, mods[l], row(ln2_g[l]), w_ff1[l].astype(BF16), w_ff3[l].astype(BF16),
                  w_ff2[l].astype(BF16), row(final_g), n_tiles, n_lat_tiles, last)
    return xa
```

```python
import functools
import math

import jax
import jax.numpy as jnp
from jax import lax
from jax.experimental import pallas as pl
from jax.experimental.pallas import tpu as pltpu

F32 = jnp.float32
BF16 = jnp.bfloat16

GRID_W = 64
MLA_HEADS = 8
QK_NOPE = 64
QK_ROPE = 32
V_HEAD = 64
Q_LORA = 384
KV_LORA = 256
CONV_CH = 256
CONV_WIDTH = 31
SC_CH = 256
SC_WIDTH = 3
N_MOD = 6
ROPE_THETA = 10000.0
EPS = 1e-6

LANES = 128
HEAD_PAD = LANES
TM = 256
HALO = 16
KC = 256
V_PAD = 80
M_INIT = -1e30
ATTN_UNROLL = 32
MOD_ROWS = 8
MOD_TN = 1536
VMEM_LIMIT = 56 * 1024 * 1024

Q_SCALE = (QK_NOPE + QK_ROPE) ** -0.5 * math.log2(math.e)

A_Q = 0
A_KV = A_Q + Q_LORA
A_GLU_A = A_KV + KV_LORA
A_GLU_G = A_GLU_A + CONV_CH
A_SC_B = A_GLU_G + CONV_CH
A_SC_C = A_SC_B + SC_CH
A_SC_X = A_SC_C + SC_CH
A_KR = A_SC_X + SC_CH
A_KR_SWAP = A_KR + HEAD_PAD
A_COLS = A_KR_SWAP + HEAD_PAD


def _dot(a, b):
    return jnp.dot(a, b, preferred_element_type=F32)


def _rms(x, g):
    return x * lax.rsqrt(jnp.mean(x * x, axis=-1, keepdims=True) + EPS) * g


def _norm_modulate(x, g, shift, scale):
    return _rms(x, g) * (1.0 + scale) + shift


def _mod_kernel(act_ref, w_ref, b_ref, o_ref):
    a = act_ref[...]
    a = a * jax.nn.sigmoid(a)
    o_ref[0] = _dot(a.astype(BF16), w_ref[0].astype(BF16)) + b_ref[0]


def _modulation(act, w_mod, b_mod):
    depth, d, n = w_mod.shape
    return pl.pallas_call(
        _mod_kernel,
        out_shape=jax.ShapeDtypeStruct((depth, MOD_ROWS, n), F32),
        grid=(depth, n // MOD_TN),
        in_specs=[
            pl.BlockSpec((MOD_ROWS, d), lambda l, j: (0, 0)),
            pl.BlockSpec((1, d, MOD_TN), lambda l, j: (l, 0, j)),
            pl.BlockSpec((1, 1, MOD_TN), lambda l, j: (l, 0, j)),
        ],
        out_specs=pl.BlockSpec((1, MOD_ROWS, MOD_TN), lambda l, j: (l, 0, j)),
        compiler_params=pltpu.CompilerParams(
            dimension_semantics=("arbitrary", "arbitrary"), vmem_limit_bytes=VMEM_LIMIT),
        name="modulation",
    )(act, w_mod, b_mod.reshape(depth, 1, n))


def _dot_t(a, b):
    return lax.dot_general(a, b, (((1,), (1,)), ((), ())), preferred_element_type=F32)


def _inproj_kernel(x_ref, mod_ref, lng_ref, wa_ref, qg_ref, wqmt_ref, wqst_ref, kvg_ref, wk_ref,
                   wvt_ref, rc_ref, rs_ref, rct_ref, rst_ref, qt_ref, k_ref, vt_ref, glu_ref, cx_ref,
                   bg_ref):
    x = x_ref[0]
    h = _norm_modulate(x, lng_ref[...], mod_ref[0, 0, 0:1, :], mod_ref[0, 0, 1:2, :])
    proj = _dot(h.astype(BF16), wa_ref[...])

    qn = _rms(proj[:, A_Q:A_Q + Q_LORA], qg_ref[...] * Q_SCALE).astype(BF16)
    qt = (_dot_t(wqmt_ref[...], qn) * jnp.concatenate([rct_ref[...]] * MLA_HEADS, axis=0)
          + _dot_t(wqst_ref[...], qn) * jnp.concatenate([rst_ref[...]] * MLA_HEADS, axis=0))
    qt_ref[0] = qt.astype(BF16)

    cn = _rms(proj[:, A_KV:A_KV + KV_LORA], kvg_ref[...]).astype(BF16)
    kr = (proj[:, A_KR:A_KR + HEAD_PAD] * rc_ref[...]
          + proj[:, A_KR_SWAP:A_KR_SWAP + HEAD_PAD] * rs_ref[...])
    k_ref[0] = (_dot(cn, wk_ref[...]) + jnp.concatenate([kr] * MLA_HEADS, axis=1)).astype(BF16)

    vt = _dot_t(wvt_ref[...], cn)
    rowid = lax.broadcasted_iota(jnp.int32, vt.shape, 0)
    vt_ref[0] = jnp.where(rowid % V_PAD == V_HEAD, 1.0, vt).astype(BF16)

    glu_ref[0] = proj[:, A_GLU_A:A_GLU_A + CONV_CH] * jax.nn.sigmoid(proj[:, A_GLU_G:A_GLU_G + CONV_CH])
    bg_ref[0] = proj[:, A_SC_B:A_SC_B + SC_CH]
    cx_ref[0] = proj[:, A_SC_C:A_SC_C + SC_CH] * proj[:, A_SC_X:A_SC_X + SC_CH]


def _inproj(xa, mods, lng, wa, qg, wqmt, wqst, kvg, wk, wvt, rc, rs, rct, rst, n_lat_tiles):
    b, t_all, d = xa.shape
    nt = t_all // TM
    hq = MLA_HEADS * HEAD_PAD
    hv = MLA_HEADS * V_PAD
    const = lambda bi, i: (0, 0)
    row = lambda bi, i: (bi, i, 0)
    col = lambda bi, i: (bi, 0, i)
    return pl.pallas_call(
        _inproj_kernel,
        out_shape=(
            jax.ShapeDtypeStruct((b, hq, t_all), BF16),
            jax.ShapeDtypeStruct((b, t_all, hq), BF16),
            jax.ShapeDtypeStruct((b, hv, t_all), BF16),
            jax.ShapeDtypeStruct((b, t_all, CONV_CH), F32),
            jax.ShapeDtypeStruct((b, t_all, SC_CH), F32),
            jax.ShapeDtypeStruct((b, t_all, SC_CH), F32),
        ),
        grid=(b, nt),
        in_specs=[
            pl.BlockSpec((1, TM, d), row),
            pl.BlockSpec((1, 1, MOD_ROWS, d), lambda bi, i: (bi, i // n_lat_tiles, 0, 0)),
            pl.BlockSpec((1, d), const),
            pl.BlockSpec((d, A_COLS), const),
            pl.BlockSpec((1, Q_LORA), const),
            pl.BlockSpec((hq, Q_LORA), const),
            pl.BlockSpec((hq, Q_LORA), const),
            pl.BlockSpec((1, KV_LORA), const),
            pl.BlockSpec((KV_LORA, hq), const),
            pl.BlockSpec((hv, KV_LORA), const),
            pl.BlockSpec((TM, HEAD_PAD), lambda bi, i: (i, 0)),
            pl.BlockSpec((TM, HEAD_PAD), lambda bi, i: (i, 0)),
            pl.BlockSpec((HEAD_PAD, TM), lambda bi, i: (0, i)),
            pl.BlockSpec((HEAD_PAD, TM), lambda bi, i: (0, i)),
        ],
        out_specs=(
            pl.BlockSpec((1, hq, TM), col),
            pl.BlockSpec((1, TM, hq), row),
            pl.BlockSpec((1, hv, TM), col),
            pl.BlockSpec((1, TM, CONV_CH), row),
            pl.BlockSpec((1, TM, SC_CH), row),
            pl.BlockSpec((1, TM, SC_CH), row),
        ),
        compiler_params=pltpu.CompilerParams(
            dimension_semantics=("arbitrary", "arbitrary"), vmem_limit_bytes=VMEM_LIMIT),
        name="inproj",
    )(xa, mods, lng, wa, qg, wqmt, wqst, kvg, wk, wvt, rc, rs, rct, rst)


def _attn_kernel(qt_ref, k_ref, vt_ref, o_ref, *, first):
    n_chunks = k_ref.shape[1] // KC
    qts = [qt_ref[0, hh * HEAD_PAD:(hh + 1) * HEAD_PAD, :] for hh in range(2)]

    def scores(kb):
        start = pl.multiple_of(kb * KC, KC)
        return tuple(_dot(k_ref[0, pl.ds(start, KC), hh * HEAD_PAD:(hh + 1) * HEAD_PAD], qts[hh])
                     for hh in range(2))

    def softmax(sts, ms):
        out = []
        for hh in range(2):
            m_new = jnp.maximum(ms[hh], jnp.max(sts[hh], axis=0, keepdims=True))
            out.append((m_new, jnp.exp2(ms[hh] - m_new), jnp.exp2(sts[hh] - m_new).astype(BF16)))
        return tuple(o[0] for o in out), tuple(o[1:] for o in out)

    def accumulate(kb, pts, accs):
        start = pl.multiple_of(jnp.maximum(kb, 0) * KC, KC)
        return tuple(
            pts[hh][0] * accs[hh] + _dot(vt_ref[0, hh * V_PAD:(hh + 1) * V_PAD, pl.ds(start, KC)],
                                         pts[hh][1])
            for hh in range(2))

    def step(kb, carry):
        sts, ms, pts, accs = carry
        ms_new, pts_new = softmax(sts, ms)
        return scores(kb), ms_new, pts_new, accumulate(kb - 2, pts, accs)

    ms = tuple(jnp.full((1, TM), M_INIT, F32) for _ in range(2))
    accs = tuple(jnp.zeros((V_PAD, TM), F32) for _ in range(2))
    pts = tuple((jnp.ones((1, TM), F32), jnp.zeros((KC, TM), BF16)) for _ in range(2))
    carry = (scores(first), ms, pts, accs)
    if n_chunks - 1 - first:
        carry = lax.fori_loop(first + 1, n_chunks, step, carry, unroll=ATTN_UNROLL)
    sts, ms, pts, accs = carry
    accs = accumulate(n_chunks - 2, pts, accs)
    _, pts = softmax(sts, ms)
    accs = accumulate(n_chunks - 1, pts, accs)
    outs = [(acc[0:V_HEAD] / acc[V_HEAD:V_HEAD + 1]).T for acc in accs]
    o_ref[0] = jnp.concatenate(outs, axis=1).astype(BF16)


def _attention(qt, k, vt, q_tile_lo, n_q_tiles, first_chunk):
    b, t_all, _ = k.shape
    pairs = MLA_HEADS // 2
    return pl.pallas_call(
        functools.partial(_attn_kernel, first=first_chunk),
        out_shape=jax.ShapeDtypeStruct((b, n_q_tiles * TM, MLA_HEADS * V_HEAD), BF16),
        grid=(b, pairs, n_q_tiles),
        in_specs=[
            pl.BlockSpec((1, 2 * HEAD_PAD, TM), lambda bi, j, i: (bi, j, i + q_tile_lo)),
            pl.BlockSpec((1, t_all, 2 * HEAD_PAD), lambda bi, j, i: (bi, 0, j)),
            pl.BlockSpec((1, 2 * V_PAD, t_all), lambda bi, j, i: (bi, j, 0)),
        ],
        out_specs=pl.BlockSpec((1, TM, 2 * V_HEAD), lambda bi, j, i: (bi, i, j)),
        compiler_params=pltpu.CompilerParams(
            dimension_semantics=("arbitrary", "arbitrary", "arbitrary"), vmem_limit_bytes=VMEM_LIMIT),
        name="attention",
    )(qt, k, vt)


def _fill_halo_buffer(buf, cur_ref, prev_ref, next_ref, keep_prev, keep_next):
    buf[0:HALO, :] = prev_ref[0] * keep_prev
    buf[HALO:HALO + TM, :] = cur_ref[0]
    buf[HALO + TM:HALO + TM + HALO, :] = next_ref[0] * keep_next


def _depthwise(buf, w_ref, width):
    base = HALO - (width - 1) // 2
    acc = w_ref[0:1, :] * buf[base:base + TM, :]
    for k in range(1, width):
        acc = acc + w_ref[k:k + 1, :] * buf[base + k:base + k + TM, :]
    return acc


def _merge_kernel(x_ref, mod_ref, lng_ref, att_ref, glu_ref, glu_p_ref, glu_n_ref, cx_ref, cx_p_ref,
                  cx_n_ref, bg_ref, wg_ref, wmo_ref, dw_ref, dwb_ref, cg_ref, cb_ref, wco_ref,
                  scw_ref, wso_ref, wo_ref, o_ref, cbuf, sbuf, *, n_lat_tiles):
    i = pl.program_id(1)
    is_ctx = i >= n_lat_tiles
    keep_prev = jnp.where((i == 0) | is_ctx, 0.0, 1.0)
    keep_next = jnp.where((i == n_lat_tiles - 1) | is_ctx, 0.0, 1.0)
    _fill_halo_buffer(cbuf, glu_ref, glu_p_ref, glu_n_ref, keep_prev, keep_next)
    _fill_halo_buffer(sbuf, cx_ref, cx_p_ref, cx_n_ref, keep_prev, keep_next)

    u = _depthwise(cbuf, dw_ref, CONV_WIDTH) + dwb_ref[...]
    mu = jnp.mean(u, axis=-1, keepdims=True)
    uc = u - mu
    var = jnp.mean(uc * uc, axis=-1, keepdims=True)
    u = uc * lax.rsqrt(var + EPS) * cg_ref[...] + cb_ref[...]
    u = u * jax.nn.sigmoid(u)
    y_conv = _dot(u.astype(BF16), wco_ref[...])

    sc = bg_ref[0] * _depthwise(sbuf, scw_ref, SC_WIDTH)
    y_sc = _dot(sc.astype(BF16), wso_ref[...])

    y_att = _dot(att_ref[0], wmo_ref[...])

    x = x_ref[0]
    d = x.shape[-1]
    h = _norm_modulate(x, lng_ref[...], mod_ref[0, 0, 0:1, :], mod_ref[0, 0, 1:2, :])
    gates = jax.nn.sigmoid(_dot(h.astype(BF16), wg_ref[...]))
    merged = gates[:, 0:d] * y_att + gates[:, d:2 * d] * y_conv + gates[:, 2 * d:3 * d] * y_sc
    o_ref[0] = x + mod_ref[0, 0, 2:3, :] * _dot(merged.astype(BF16), wo_ref[...])


def _merge(xa, mods, lng, att, glu, cx, bg, wg, wmo, dw, dwb, cg, cb, wco, scw, wso, wo, n_tiles,
           n_lat_tiles):
    b, t_all, d = xa.shape
    per = TM // HALO
    last_halo = t_all // HALO - 1
    const = lambda bi, i: (0, 0)
    row = lambda bi, i: (bi, i, 0)
    prev = lambda bi, i: (bi, jnp.maximum(i * per - 1, 0), 0)
    nxt = lambda bi, i: (bi, jnp.minimum((i + 1) * per, last_halo), 0)
    hv = MLA_HEADS * V_HEAD
    return pl.pallas_call(
        functools.partial(_merge_kernel, n_lat_tiles=n_lat_tiles),
        out_shape=jax.ShapeDtypeStruct((b, n_tiles * TM, d), F32),
        grid=(b, n_tiles),
        in_specs=[
            pl.BlockSpec((1, TM, d), row),
            pl.BlockSpec((1, 1, MOD_ROWS, d), lambda bi, i: (bi, i // n_lat_tiles, 0, 0)),
            pl.BlockSpec((1, d), const),
            pl.BlockSpec((1, TM, hv), row),
            pl.BlockSpec((1, TM, CONV_CH), row),
            pl.BlockSpec((1, HALO, CONV_CH), prev),
            pl.BlockSpec((1, HALO, CONV_CH), nxt),
            pl.BlockSpec((1, TM, SC_CH), row),
            pl.BlockSpec((1, HALO, SC_CH), prev),
            pl.BlockSpec((1, HALO, SC_CH), nxt),
            pl.BlockSpec((1, TM, SC_CH), row),
            pl.BlockSpec((d, 3 * d), const),
            pl.BlockSpec((hv, d), const),
            pl.BlockSpec((CONV_WIDTH, CONV_CH), const),
            pl.BlockSpec((1, CONV_CH), const),
            pl.BlockSpec((1, CONV_CH), const),
            pl.BlockSpec((1, CONV_CH), const),
            pl.BlockSpec((CONV_CH, d), const),
            pl.BlockSpec((SC_WIDTH, SC_CH), const),
            pl.BlockSpec((SC_CH, d), const),
            pl.BlockSpec((d, d), const),
        ],
        out_specs=pl.BlockSpec((1, TM, d), row),
        scratch_shapes=[
            pltpu.VMEM((TM + 2 * HALO, CONV_CH), F32),
            pltpu.VMEM((TM + 2 * HALO, SC_CH), F32),
        ],
        compiler_params=pltpu.CompilerParams(
            dimension_semantics=("arbitrary", "arbitrary"), vmem_limit_bytes=VMEM_LIMIT),
        name="merge",
    )(xa, mods, lng, att, glu, glu, glu, cx, cx, cx, bg, wg, wmo, dw, dwb, cg, cb, wco, scw, wso, wo)


def _ffn_kernel(x_ref, mod_ref, lng_ref, w1_ref, w3_ref, w2_ref, fg_ref, o_ref, *, final_norm):
    x = x_ref[0]
    h = _norm_modulate(x, lng_ref[...], mod_ref[0, 0, 3:4, :], mod_ref[0, 0, 4:5, :]).astype(BF16)
    a = _dot(h, w1_ref[...])
    a = a * jax.nn.sigmoid(a) * _dot(h, w3_ref[...])
    y = x + mod_ref[0, 0, 5:6, :] * _dot(a.astype(BF16), w2_ref[...])
    if final_norm:
        y = _rms(y, fg_ref[...])
    o_ref[0] = y


def _ffn(xa, mods, lng, w1, w3, w2, fg, n_tiles, n_lat_tiles, final_norm):
    b, t_all, d = xa.shape
    dff = w1.shape[1]
    const = lambda bi, i: (0, 0)
    row = lambda bi, i: (bi, i, 0)
    return pl.pallas_call(
        functools.partial(_ffn_kernel, final_norm=final_norm),
        out_shape=jax.ShapeDtypeStruct((b, n_tiles * TM, d), F32),
        grid=(b, n_tiles),
        in_specs=[
            pl.BlockSpec((1, TM, d), row),
            pl.BlockSpec((1, 1, MOD_ROWS, d), lambda bi, i: (bi, i // n_lat_tiles, 0, 0)),
            pl.BlockSpec((1, d), const),
            pl.BlockSpec((d, dff), const),
            pl.BlockSpec((d, dff), const),
            pl.BlockSpec((dff, d), const),
            pl.BlockSpec((1, d), const),
        ],
        out_specs=pl.BlockSpec((1, TM, d), row),
        compiler_params=pltpu.CompilerParams(
            dimension_semantics=("arbitrary", "arbitrary"), vmem_limit_bytes=VMEM_LIMIT),
        name="ffn",
    )(xa, mods, lng, w1, w3, w2, fg)


def _rope_patterns(n_lat, n_ctx):
    t = jnp.arange(n_lat)
    row = (t // GRID_W).astype(F32)
    col = (t % GRID_W).astype(F32)
    half = QK_ROPE // 2
    inv = jnp.power(jnp.float32(ROPE_THETA), -jnp.arange(0, half, 2, dtype=F32) / half)
    ang_r = row[:, None] * inv
    ang_c = col[:, None] * inv
    cos = jnp.concatenate([jnp.cos(ang_r)] * 2 + [jnp.cos(ang_c)] * 2, axis=1)
    sin = jnp.concatenate([-jnp.sin(ang_r), jnp.sin(ang_r), -jnp.sin(ang_c), jnp.sin(ang_c)], axis=1)
    pad = HEAD_PAD - QK_NOPE - QK_ROPE
    rc = jnp.concatenate([jnp.ones((n_lat, QK_NOPE), F32), cos, jnp.zeros((n_lat, pad), F32)], axis=1)
    rs = jnp.concatenate([jnp.zeros((n_lat, QK_NOPE), F32), sin, jnp.zeros((n_lat, pad), F32)], axis=1)
    rc_ctx = jnp.concatenate([jnp.ones((n_ctx, QK_NOPE + QK_ROPE), F32), jnp.zeros((n_ctx, pad), F32)], axis=1)
    return (jnp.concatenate([rc, rc_ctx], axis=0),
            jnp.concatenate([rs, jnp.zeros((n_ctx, HEAD_PAD), F32)], axis=0))


def _swap_rope_columns(w):
    idx = jnp.arange(QK_ROPE) ^ (QK_ROPE // 4)
    return w[..., idx]


def _layer_weights(w_in, w_q_b, w_kv_b):
    d = w_in.shape[0]
    q_end = Q_LORA
    kv_end = q_end + KV_LORA + QK_ROPE
    sc_end = kv_end + 2 * CONV_CH + 3 * SC_CH
    w_kr = w_in[:, q_end + KV_LORA:kv_end]
    pad_lo = jnp.zeros((d, QK_NOPE), w_in.dtype)
    pad_hi = jnp.zeros((d, HEAD_PAD - QK_NOPE - QK_ROPE), w_in.dtype)
    wa = jnp.concatenate([
        w_in[:, :q_end + KV_LORA], w_in[:, kv_end:sc_end],
        pad_lo, w_kr, pad_hi, pad_lo, _swap_rope_columns(w_kr), pad_hi], axis=1).astype(BF16)
    wg = w_in[:, sc_end:].astype(BF16)

    wq = w_q_b.reshape(Q_LORA, MLA_HEADS, QK_NOPE + QK_ROPE)
    zq = jnp.zeros((Q_LORA, MLA_HEADS, HEAD_PAD - QK_NOPE - QK_ROPE), w_q_b.dtype)
    wqm = jnp.concatenate([wq, zq], axis=2).reshape(Q_LORA, MLA_HEADS * HEAD_PAD)
    wqs = jnp.concatenate([jnp.zeros((Q_LORA, MLA_HEADS, QK_NOPE), w_q_b.dtype),
                           _swap_rope_columns(wq[:, :, QK_NOPE:]), zq], axis=2)
    wqs = wqs.reshape(Q_LORA, MLA_HEADS * HEAD_PAD)

    wkv = w_kv_b.reshape(KV_LORA, MLA_HEADS, QK_NOPE + V_HEAD)
    wk = jnp.concatenate([wkv[:, :, :QK_NOPE],
                          jnp.zeros((KV_LORA, MLA_HEADS, HEAD_PAD - QK_NOPE), w_kv_b.dtype)], axis=2)
    wk = wk.reshape(KV_LORA, MLA_HEADS * HEAD_PAD).astype(BF16)
    wv = jnp.concatenate([wkv[:, :, QK_NOPE:],
                          jnp.zeros((KV_LORA, MLA_HEADS, V_PAD - V_HEAD), w_kv_b.dtype)], axis=2)
    wvt = wv.reshape(KV_LORA, MLA_HEADS * V_PAD).T.astype(BF16)
    return wa, wg, wqm.T.astype(BF16), wqs.T.astype(BF16), wk, wvt


def kernel(x, c, ctx, c_ctx, w_mod, b_mod, ln1_g, w_in, q_a_norm_g, w_q_b, kv_a_norm_g, w_kv_b, w_mla_o, conv_dw, conv_dw_b, conv_ln_g, conv_ln_b, w_conv_o, sc_dw, w_sc_o, w_o, ln2_g, w_ff1, w_ff3, w_ff2, final_g):
    b, n_lat, d = x.shape
    n_ctx = ctx.shape[1]
    depth = w_mod.shape[0]
    assert n_ctx % TM == 0 and n_lat % TM == 0 and TM % KC == 0 and b + 1 <= MOD_ROWS
    n_lat_tiles = n_lat // TM
    n_all_tiles = (n_lat + n_ctx) // TM

    act = jnp.concatenate([c, c_ctx[None, :], jnp.zeros((MOD_ROWS - b - 1, d), F32)], axis=0)
    mod = _modulation(act, w_mod, b_mod).reshape(depth, MOD_ROWS, N_MOD, d)
    mod = jnp.pad(mod, ((0, 0), (0, 0), (0, MOD_ROWS - N_MOD), (0, 0)))
    mod_ctx = jnp.broadcast_to(mod[:, b][:, None], (depth, b, MOD_ROWS, d))
    mods = jnp.stack([mod[:, :b], mod_ctx], axis=2)

    rc, rs = _rope_patterns(n_lat, n_ctx)
    xa = jnp.concatenate([x, ctx], axis=1)
    row = lambda v: v.reshape(1, -1)

    for l in range(depth):
        last = l == depth - 1
        wa, wg, wqmt, wqst, wk, wvt = _layer_weights(w_in[l], w_q_b[l], w_kv_b[l])
        qt, k, vt, glu, cx, bg = _inproj(
            xa, mods[l], row(ln1_g[l]), wa, row(q_a_norm_g[l]), wqmt, wqst, row(kv_a_norm_g[l]),
            wk, wvt, rc, rs, rc.T, rs.T, n_lat_tiles)
        n_tiles = n_lat_tiles if last else n_all_tiles
        att = _attention(qt, k, vt, 0, n_lat_tiles, 0)
        if not last:
            att_ctx = _attention(qt, k, vt, n_lat_tiles, n_all_tiles - n_lat_tiles, n_lat // KC)
            att = jnp.concatenate([att, att_ctx], axis=1)
        xa = _merge(xa, mods[l], row(ln1_g[l]), att, glu, cx, bg, wg, w_mla_o[l].astype(BF16),
                    conv_dw[l], row(conv_dw_b[l]), row(conv_ln_g[l]), row(conv_ln_b[l]),
                    w_conv_o[l].astype(BF16), sc_dw[l], w_sc_o[l].astype(BF16), w_o[l].astype(BF16),
                    n_tiles, n_lat_tiles)
        xa = _ffn(xa, mods[l], row(ln2_g[l]), w_ff1[l].astype(BF16), w_ff3[l].astype(BF16),
                  w_ff2[l].astype(BF16), row(final_g), n_tiles, n_lat_tiles, last)
    return xa
```

```python
import functools
import math

import jax
import jax.numpy as jnp
import numpy as np
from jax import lax
from jax.experimental import pallas as pl
from jax.experimental.pallas import tpu as pltpu

F32 = jnp.float32
BF16 = jnp.bfloat16

GRID_W = 64
MLA_HEADS = 8
QK_NOPE = 64
QK_ROPE = 32
V_HEAD = 64
Q_LORA = 384
KV_LORA = 256
CONV_CH = 256
CONV_WIDTH = 31
SC_CH = 256
SC_WIDTH = 3
N_MOD = 6
ROPE_THETA = 10000.0
EPS = 1e-6

LANES = 128
SUBLANES = 8
HEAD_PAD = LANES
TM = 256
HALO = 16
KC = 256
V_PAD = 80
M_INIT = -1e30
ATTN_UNROLL = 32
ATTN_HEADS = 2
MOD_ROWS = 8
MOD_TN = 1536
VMEM_LIMIT = 56 * 1024 * 1024

Q_SCALE = (QK_NOPE + QK_ROPE) ** -0.5 * math.log2(math.e)

A_Q = 0
A_KV = A_Q + Q_LORA
A_GLU_A = A_KV + KV_LORA
A_GLU_G = A_GLU_A + CONV_CH
A_SC_B = A_GLU_G + CONV_CH
A_SC_C = A_SC_B + SC_CH
A_SC_X = A_SC_C + SC_CH
A_KR = A_SC_X + SC_CH
A_KR_SWAP = A_KR + HEAD_PAD
A_COLS = A_KR_SWAP + HEAD_PAD


def _dot(a, b):
    return jnp.dot(a, b, preferred_element_type=F32)


def _rms(x, g):
    return x * lax.rsqrt(jnp.mean(x * x, axis=-1, keepdims=True) + EPS) * g


def _norm_modulate(x, g, shift, scale):
    return _rms(x, g) * (1.0 + scale) + shift


def _sigmoid(x):
    return 0.5 * jnp.tanh(0.5 * x) + 0.5


def _pad_cols(w):
    return jnp.pad(w, ((0, 0),) * (w.ndim - 1) + ((0, LANES),)).astype(BF16)


def _mod_kernel(act_ref, w_ref, b_ref, o_ref):
    a = act_ref[...]
    a = a * _sigmoid(a)
    o_ref[0] = _dot(a.astype(BF16), w_ref[0].astype(BF16)) + b_ref[0]


def _modulation(act, w_mod, b_mod):
    depth, d, n = w_mod.shape
    return pl.pallas_call(
        _mod_kernel,
        out_shape=jax.ShapeDtypeStruct((depth, MOD_ROWS, n), F32),
        grid=(depth, n // MOD_TN),
        in_specs=[
            pl.BlockSpec((MOD_ROWS, d), lambda l, j: (0, 0)),
            pl.BlockSpec((1, d, MOD_TN), lambda l, j: (l, 0, j)),
            pl.BlockSpec((1, 1, MOD_TN), lambda l, j: (l, 0, j)),
        ],
        out_specs=pl.BlockSpec((1, MOD_ROWS, MOD_TN), lambda l, j: (l, 0, j)),
        compiler_params=pltpu.CompilerParams(
            dimension_semantics=("arbitrary", "arbitrary"), vmem_limit_bytes=VMEM_LIMIT),
        name="modulation",
    )(act, w_mod, b_mod.reshape(depth, 1, n))


def _dot_t(a, b):
    return lax.dot_general(a, b, (((1,), (1,)), ((), ())), preferred_element_type=F32)


def _tile_rows(lat_ref, ctx_ref, n_lat_tiles):
    if ctx_ref is None:
        return lat_ref[0]
    return jnp.where(pl.program_id(1) >= n_lat_tiles, ctx_ref[0], lat_ref[0])


def _inproj_kernel(*refs, n_lat_tiles, split):
    x_ref, ctx_ref = (refs[0], refs[1]) if split else (refs[0], None)
    (mod_ref, lng_ref, wa_ref, qg_ref, wqmt_ref, wqst_ref, kvg_ref, wk_ref, wvt_ref, rc_ref, rs_ref,
     rct_ref, rst_ref, qt_ref, k_ref, vt_ref, glu_ref, cx_ref, bg_ref) = refs[2 if split else 1:]
    x = _tile_rows(x_ref, ctx_ref, n_lat_tiles)
    h = _norm_modulate(x, lng_ref[...], mod_ref[0:1, :], mod_ref[1:2, :])
    proj = _dot(h.astype(BF16), wa_ref[...])

    qn = _rms(proj[:, A_Q:A_Q + Q_LORA], qg_ref[...] * Q_SCALE).astype(BF16)
    qt = (_dot_t(wqmt_ref[...], qn) * jnp.concatenate([rct_ref[...]] * MLA_HEADS, axis=0)
          + _dot_t(wqst_ref[...], qn) * jnp.concatenate([rst_ref[...]] * MLA_HEADS, axis=0))
    qt_ref[0] = qt.astype(BF16)

    cn = _rms(proj[:, A_KV:A_KV + KV_LORA], kvg_ref[...]).astype(BF16)
    kr = (proj[:, A_KR:A_KR + HEAD_PAD] * rc_ref[...]
          + proj[:, A_KR_SWAP:A_KR_SWAP + HEAD_PAD] * rs_ref[...])
    k_ref[0] = (_dot(cn, wk_ref[...]) + jnp.concatenate([kr] * MLA_HEADS, axis=1)).astype(BF16)

    vt = _dot_t(wvt_ref[...], cn)
    rowid = lax.broadcasted_iota(jnp.int32, vt.shape, 0)
    vt_ref[0] = jnp.where(rowid % V_PAD == V_HEAD, 1.0, vt).astype(BF16)

    glu_ref[0] = proj[:, A_GLU_A:A_GLU_A + CONV_CH] * _sigmoid(proj[:, A_GLU_G:A_GLU_G + CONV_CH])
    bg_ref[0] = proj[:, A_SC_B:A_SC_B + SC_CH]
    cx_ref[0] = proj[:, A_SC_C:A_SC_C + SC_CH] * proj[:, A_SC_X:A_SC_X + SC_CH]


def _stream_specs(arrs, n_lat_tiles):
    width = arrs[0].shape[-1]
    if len(arrs) == 1:
        return [pl.BlockSpec((1, TM, width), lambda bi, i: (bi, i, 0))]
    assert arrs[1].shape[1] == TM
    return [pl.BlockSpec((1, TM, width), lambda bi, i: (bi, jnp.minimum(i, n_lat_tiles - 1), 0)),
            pl.BlockSpec((1, TM, width), lambda bi, i: (bi, 0, 0))]


def _layer_spec(l, arr):
    zeros = (0,) * (arr.ndim - 1)
    return pl.BlockSpec((None,) + arr.shape[1:], lambda bi, i: (l,) + zeros)


def _mod_spec(l, mods, n_lat_tiles, ctx_row):
    _, _, n_mod, d = mods.shape
    return pl.BlockSpec((None, None, n_mod, d),
                        lambda bi, i: (l, jnp.where(i < n_lat_tiles, bi, ctx_row), 0, 0))


def _inproj(l, xs, mods, lng, wa, qg, wqmt, wqst, kvg, wk, wvt, rc, rs, rct, rst, n_tiles,
            n_lat_tiles):
    b, _, d = xs[0].shape
    t_all = n_tiles * TM
    hq = MLA_HEADS * HEAD_PAD
    hv = MLA_HEADS * V_PAD
    row = lambda bi, i: (bi, i, 0)
    col = lambda bi, i: (bi, 0, i)
    params = (lng, wa, qg, wqmt, wqst, kvg, wk, wvt)
    return pl.pallas_call(
        functools.partial(_inproj_kernel, n_lat_tiles=n_lat_tiles, split=len(xs) == 2),
        out_shape=(
            jax.ShapeDtypeStruct((b, hq, t_all), BF16),
            jax.ShapeDtypeStruct((b, t_all, hq), BF16),
            jax.ShapeDtypeStruct((b, hv, t_all), BF16),
            jax.ShapeDtypeStruct((b, t_all, CONV_CH), F32),
            jax.ShapeDtypeStruct((b, t_all, SC_CH), F32),
            jax.ShapeDtypeStruct((b, t_all, SC_CH), F32),
        ),
        grid=(b, n_tiles),
        in_specs=(
            _stream_specs(xs, n_lat_tiles)
            + [_mod_spec(l, mods, n_lat_tiles, b)]
            + [_layer_spec(l, p) for p in params]
            + [pl.BlockSpec((TM, HEAD_PAD), lambda bi, i: (i, 0)),
               pl.BlockSpec((TM, HEAD_PAD), lambda bi, i: (i, 0)),
               pl.BlockSpec((HEAD_PAD, TM), lambda bi, i: (0, i)),
               pl.BlockSpec((HEAD_PAD, TM), lambda bi, i: (0, i))]),
        out_specs=(
            pl.BlockSpec((1, hq, TM), col),
            pl.BlockSpec((1, TM, hq), row),
            pl.BlockSpec((1, hv, TM), col),
            pl.BlockSpec((1, TM, CONV_CH), row),
            pl.BlockSpec((1, TM, SC_CH), row),
            pl.BlockSpec((1, TM, SC_CH), row),
        ),
        compiler_params=pltpu.CompilerParams(
            dimension_semantics=("arbitrary", "arbitrary"), vmem_limit_bytes=VMEM_LIMIT),
        name="inproj",
    )(*xs, mods, *params, rc, rs, rct, rst)


def _attn_kernel(qt_ref, k_ref, vt_ref, o_ref, *, first):
    n_chunks = k_ref.shape[1] // KC
    heads = range(ATTN_HEADS)
    qts = [qt_ref[0, hh * HEAD_PAD:(hh + 1) * HEAD_PAD, :] for hh in heads]

    def scores(kb):
        start = pl.multiple_of(kb * KC, KC)
        return tuple(_dot(k_ref[0, pl.ds(start, KC), hh * HEAD_PAD:(hh + 1) * HEAD_PAD], qts[hh])
                     for hh in heads)

    def softmax(sts, ms):
        out = []
        for hh in heads:
            m_new = jnp.maximum(ms[hh], jnp.max(sts[hh], axis=0, keepdims=True))
            out.append((m_new, jnp.exp2(ms[hh] - m_new), jnp.exp2(sts[hh] - m_new).astype(BF16)))
        return tuple(o[0] for o in out), tuple(o[1:] for o in out)

    def accumulate(kb, pts, accs):
        start = pl.multiple_of(jnp.maximum(kb, 0) * KC, KC)
        return tuple(
            pts[hh][0] * accs[hh] + _dot(vt_ref[0, hh * V_PAD:(hh + 1) * V_PAD, pl.ds(start, KC)],
                                         pts[hh][1])
            for hh in heads)

    def step(kb, carry):
        sts, ms, pts, accs = carry
        ms_new, pts_new = softmax(sts, ms)
        return scores(kb), ms_new, pts_new, accumulate(kb - 2, pts, accs)

    ms = tuple(jnp.full((1, TM), M_INIT, F32) for _ in heads)
    accs = tuple(jnp.zeros((V_PAD, TM), F32) for _ in heads)
    pts = tuple((jnp.ones((1, TM), F32), jnp.zeros((KC, TM), BF16)) for _ in heads)
    carry = (scores(first), ms, pts, accs)
    if n_chunks - 1 - first:
        carry = lax.fori_loop(first + 1, n_chunks, step, carry, unroll=ATTN_UNROLL)
    sts, ms, pts, accs = carry
    accs = accumulate(n_chunks - 2, pts, accs)
    _, pts = softmax(sts, ms)
    accs = accumulate(n_chunks - 1, pts, accs)
    outs = [(acc[0:V_HEAD] / acc[V_HEAD:V_HEAD + 1]).T for acc in accs]
    o_ref[0] = jnp.concatenate(outs, axis=1).astype(BF16)


def _attention(qt, k, vt, q_tile_lo, n_q_tiles, first_chunk):
    b, t_all, _ = k.shape
    groups = MLA_HEADS // ATTN_HEADS
    return pl.pallas_call(
        functools.partial(_attn_kernel, first=first_chunk),
        out_shape=jax.ShapeDtypeStruct((b, n_q_tiles * TM, MLA_HEADS * V_HEAD), BF16),
        grid=(b, groups, n_q_tiles),
        in_specs=[
            pl.BlockSpec((1, ATTN_HEADS * HEAD_PAD, TM), lambda bi, j, i: (bi, j, i + q_tile_lo)),
            pl.BlockSpec((1, t_all, ATTN_HEADS * HEAD_PAD), lambda bi, j, i: (bi, 0, j)),
            pl.BlockSpec((1, ATTN_HEADS * V_PAD, t_all), lambda bi, j, i: (bi, j, 0)),
        ],
        out_specs=pl.BlockSpec((1, TM, ATTN_HEADS * V_HEAD), lambda bi, j, i: (bi, i, j)),
        compiler_params=pltpu.CompilerParams(
            dimension_semantics=("arbitrary", "arbitrary", "arbitrary"), vmem_limit_bytes=VMEM_LIMIT),
        name="attention",
    )(qt, k, vt)


def _fill_halo_buffer(buf, cur_ref, prev_ref, next_ref, keep_prev, keep_next):
    buf[0:HALO, :] = prev_ref[0] * keep_prev
    buf[HALO:HALO + TM, :] = cur_ref[0]
    buf[HALO + TM:HALO + TM + HALO, :] = next_ref[0] * keep_next


def _depthwise(buf, w_ref, width):
    base = HALO - (width - 1) // 2
    out = None
    for r in range(SUBLANES):
        acc = None
        for k in range(width):
            if (base + k) % SUBLANES != r:
                continue
            off = base + k - r
            term = w_ref[k:k + 1, :] * buf[off:off + TM + SUBLANES, :]
            acc = term if acc is None else acc + term
        if acc is not None:
            out = acc[r:r + TM, :] if out is None else out + acc[r:r + TM, :]
    return out


def _merge_kernel(*refs, n_lat_tiles, x_split, att_split):
    refs = list(refs)
    x_ref = refs.pop(0)
    xc_ref = refs.pop(0) if x_split else None
    att_ref = refs.pop(0)
    attc_ref = refs.pop(0) if att_split else None
    (mod_ref, glu_ref, glu_p_ref, glu_n_ref, cx_ref, cx_p_ref, cx_n_ref, bg_ref, lng_ref, wg_ref,
     wmo_ref, dw_ref, dwb_ref, cg_ref, cb_ref, wco_ref, scw_ref, wso_ref, wo_ref, o_ref, cbuf,
     sbuf) = refs
    i = pl.program_id(1)
    is_ctx = i >= n_lat_tiles
    keep_prev = jnp.where((i == 0) | is_ctx, 0.0, 1.0)
    keep_next = jnp.where((i == n_lat_tiles - 1) | is_ctx, 0.0, 1.0)
    _fill_halo_buffer(cbuf, glu_ref, glu_p_ref, glu_n_ref, keep_prev, keep_next)
    _fill_halo_buffer(sbuf, cx_ref, cx_p_ref, cx_n_ref, keep_prev, keep_next)

    x = _tile_rows(x_ref, xc_ref, n_lat_tiles)
    d = x.shape[-1]
    h = _norm_modulate(x, lng_ref[...], mod_ref[0:1, :], mod_ref[1:2, :]).astype(BF16)

    def gate(branch):
        return _sigmoid(_dot(h, wg_ref[:, branch * d:(branch + 1) * d]))

    merged = gate(0) * _dot(_tile_rows(att_ref, attc_ref, n_lat_tiles), wmo_ref[:, 0:d])

    u = _depthwise(cbuf, dw_ref, CONV_WIDTH) + dwb_ref[...]
    mu = jnp.mean(u, axis=-1, keepdims=True)
    uc = u - mu
    var = jnp.mean(uc * uc, axis=-1, keepdims=True)
    u = uc * lax.rsqrt(var + EPS) * cg_ref[...] + cb_ref[...]
    u = u * _sigmoid(u)
    merged = merged + gate(1) * _dot(u.astype(BF16), wco_ref[:, 0:d])

    sc = bg_ref[0] * _depthwise(sbuf, scw_ref, SC_WIDTH)
    merged = merged + gate(2) * _dot(sc.astype(BF16), wso_ref[:, 0:d])

    o_ref[0] = x + mod_ref[2:3, :] * _dot(merged.astype(BF16), wo_ref[:, 0:d])


def _merge(l, xs, atts, mods, glu, cx, bg, lng, wg, wmo, dw, dwb, cg, cb, wco, scw, wso, wo, n_tiles,
           n_lat_tiles):
    b, _, d = xs[0].shape
    per = TM // HALO
    last_halo = glu.shape[1] // HALO - 1
    row = lambda bi, i: (bi, i, 0)
    prev = lambda bi, i: (bi, jnp.maximum(i * per - 1, 0), 0)
    nxt = lambda bi, i: (bi, jnp.minimum((i + 1) * per, last_halo), 0)
    params = (lng, wg, wmo, dw, dwb, cg, cb, wco, scw, wso, wo)
    return pl.pallas_call(
        functools.partial(_merge_kernel, n_lat_tiles=n_lat_tiles, x_split=len(xs) == 2,
                          att_split=len(atts) == 2),
        out_shape=jax.ShapeDtypeStruct((b, n_tiles * TM, d), F32),
        grid=(b, n_tiles),
        in_specs=(
            _stream_specs(xs, n_lat_tiles)
            + _stream_specs(atts, n_lat_tiles)
            + [_mod_spec(l, mods, n_lat_tiles, b),
               pl.BlockSpec((1, TM, CONV_CH), row),
               pl.BlockSpec((1, HALO, CONV_CH), prev),
               pl.BlockSpec((1, HALO, CONV_CH), nxt),
               pl.BlockSpec((1, TM, SC_CH), row),
               pl.BlockSpec((1, HALO, SC_CH), prev),
               pl.BlockSpec((1, HALO, SC_CH), nxt),
               pl.BlockSpec((1, TM, SC_CH), row)]
            + [_layer_spec(l, p) for p in params]),
        out_specs=pl.BlockSpec((1, TM, d), row),
        scratch_shapes=[
            pltpu.VMEM((TM + 2 * HALO, CONV_CH), F32),
            pltpu.VMEM((TM + 2 * HALO, SC_CH), F32),
        ],
        compiler_params=pltpu.CompilerParams(
            dimension_semantics=("arbitrary", "arbitrary"), vmem_limit_bytes=VMEM_LIMIT),
        name="merge",
    )(*xs, *atts, mods, glu, glu, glu, cx, cx, cx, bg, *params)


def _ffn_kernel(x_ref, mod_ref, lng_ref, w1_ref, w3_ref, w2_ref, fg_ref, o_ref, *, final_norm):
    x = x_ref[0]
    h = _norm_modulate(x, lng_ref[...], mod_ref[3:4, :], mod_ref[4:5, :]).astype(BF16)
    a = _dot(h, w1_ref[...])
    a = a * _sigmoid(a) * _dot(h, w3_ref[...])
    y = x + mod_ref[5:6, :] * _dot(a.astype(BF16), w2_ref[...])
    if final_norm:
        y = _rms(y, fg_ref[...])
    o_ref[0] = y


def _ffn(l, xa, mods, lng, w1, w3, w2, fg, n_tiles, n_lat_tiles, final_norm):
    b, _, d = xa.shape
    row = lambda bi, i: (bi, i, 0)
    return pl.pallas_call(
        functools.partial(_ffn_kernel, final_norm=final_norm),
        out_shape=jax.ShapeDtypeStruct((b, n_tiles * TM, d), F32),
        grid=(b, n_tiles),
        in_specs=[
            pl.BlockSpec((1, TM, d), row),
            _mod_spec(l, mods, n_lat_tiles, b),
            _layer_spec(l, lng),
            _layer_spec(l, w1),
            _layer_spec(l, w3),
            _layer_spec(l, w2),
            pl.BlockSpec((1, d), lambda bi, i: (0, 0)),
        ],
        out_specs=pl.BlockSpec((1, TM, d), row),
        compiler_params=pltpu.CompilerParams(
            dimension_semantics=("arbitrary", "arbitrary"), vmem_limit_bytes=VMEM_LIMIT),
        name="ffn",
    )(xa, mods, lng, w1, w3, w2, fg)


def _rope_patterns(n_lat, n_ctx):
    t = np.arange(n_lat)
    half = QK_ROPE // 2
    inv = np.float64(ROPE_THETA) ** (-np.arange(0, half, 2, dtype=np.float64) / half)
    ang_r = (t // GRID_W)[:, None] * inv
    ang_c = (t % GRID_W)[:, None] * inv
    cos = np.concatenate([np.cos(ang_r)] * 2 + [np.cos(ang_c)] * 2, axis=1)
    sin = np.concatenate([-np.sin(ang_r), np.sin(ang_r), -np.sin(ang_c), np.sin(ang_c)], axis=1)
    pad = HEAD_PAD - QK_NOPE - QK_ROPE
    rc = np.concatenate([np.ones((n_lat, QK_NOPE)), cos, np.zeros((n_lat, pad))], axis=1)
    rs = np.concatenate([np.zeros((n_lat, QK_NOPE)), sin, np.zeros((n_lat, pad))], axis=1)
    rc_ctx = np.concatenate([np.ones((n_ctx, QK_NOPE + QK_ROPE)), np.zeros((n_ctx, pad))], axis=1)
    rc = np.concatenate([rc, rc_ctx], axis=0).astype(np.float32)
    rs = np.concatenate([rs, np.zeros((n_ctx, HEAD_PAD))], axis=0).astype(np.float32)
    return rc, rs, np.ascontiguousarray(rc.T), np.ascontiguousarray(rs.T)


def _swap_rope_columns(w):
    idx = jnp.arange(QK_ROPE) ^ (QK_ROPE // 4)
    return w[..., idx]


def _projection_weights(w_in, w_q_b, w_kv_b):
    depth, d, _ = w_in.shape
    q_end = Q_LORA
    kv_end = q_end + KV_LORA + QK_ROPE
    sc_end = kv_end + 2 * CONV_CH + 3 * SC_CH
    w_kr = w_in[..., q_end + KV_LORA:kv_end]
    pad_lo = jnp.zeros((depth, d, QK_NOPE), w_in.dtype)
    pad_hi = jnp.zeros((depth, d, HEAD_PAD - QK_NOPE - QK_ROPE), w_in.dtype)
    wa = jnp.concatenate([
        w_in[..., :q_end + KV_LORA], w_in[..., kv_end:sc_end],
        pad_lo, w_kr, pad_hi, pad_lo, _swap_rope_columns(w_kr), pad_hi], axis=-1).astype(BF16)
    wg = _pad_cols(w_in[..., sc_end:])

    wq = w_q_b.reshape(depth, Q_LORA, MLA_HEADS, QK_NOPE + QK_ROPE)
    zq = jnp.zeros((depth, Q_LORA, MLA_HEADS, HEAD_PAD - QK_NOPE - QK_ROPE), w_q_b.dtype)
    wqm = jnp.concatenate([wq, zq], axis=-1).reshape(depth, Q_LORA, MLA_HEADS * HEAD_PAD)
    wqs = jnp.concatenate([jnp.zeros((depth, Q_LORA, MLA_HEADS, QK_NOPE), w_q_b.dtype),
                           _swap_rope_columns(wq[..., QK_NOPE:]), zq], axis=-1)
    wqs = wqs.reshape(depth, Q_LORA, MLA_HEADS * HEAD_PAD)

    wkv = w_kv_b.reshape(depth, KV_LORA, MLA_HEADS, QK_NOPE + V_HEAD)
    wk = jnp.concatenate(
        [wkv[..., :QK_NOPE],
         jnp.zeros((depth, KV_LORA, MLA_HEADS, HEAD_PAD - QK_NOPE), w_kv_b.dtype)], axis=-1)
    wk = wk.reshape(depth, KV_LORA, MLA_HEADS * HEAD_PAD).astype(BF16)
    wv = jnp.concatenate(
        [wkv[..., QK_NOPE:],
         jnp.zeros((depth, KV_LORA, MLA_HEADS, V_PAD - V_HEAD), w_kv_b.dtype)], axis=-1)
    wvt = jnp.swapaxes(wv.reshape(depth, KV_LORA, MLA_HEADS * V_PAD), 1, 2).astype(BF16)
    return (wa, wg, jnp.swapaxes(wqm, 1, 2).astype(BF16), jnp.swapaxes(wqs, 1, 2).astype(BF16),
            wk, wvt)


def kernel(x, c, ctx, c_ctx, w_mod, b_mod, ln1_g, w_in, q_a_norm_g, w_q_b, kv_a_norm_g, w_kv_b, w_mla_o, conv_dw, conv_dw_b, conv_ln_g, conv_ln_b, w_conv_o, sc_dw, w_sc_o, w_o, ln2_g, w_ff1, w_ff3, w_ff2, final_g):
    b, n_lat, d = x.shape
    n_ctx = ctx.shape[1]
    depth = w_mod.shape[0]
    assert n_ctx == TM and n_lat % TM == 0 and TM % KC == 0 and b + 1 <= MOD_ROWS
    n_lat_tiles = n_lat // TM
    n_all_tiles = (n_lat + n_ctx) // TM

    act = jnp.concatenate([c, c_ctx[None, :], jnp.zeros((MOD_ROWS - b - 1, d), F32)], axis=0)
    mods = _modulation(act, w_mod, b_mod).reshape(depth, MOD_ROWS, N_MOD, d)

    rc, rs, rct, rst = _rope_patterns(n_lat, n_ctx)
    vec = lambda v: v.reshape(depth, 1, -1)
    wa, wg, wqmt, wqst, wk, wvt = _projection_weights(w_in, w_q_b, w_kv_b)
    wmo, wco, wso, wo = (_pad_cols(w) for w in (w_mla_o, w_conv_o, w_sc_o, w_o))
    w1, w3, w2 = (w.astype(BF16) for w in (w_ff1, w_ff3, w_ff2))

    xs = (x, ctx)
    for l in range(depth):
        last = l == depth - 1
        qt, k, vt, glu, cx, bg = _inproj(
            l, xs, mods, vec(ln1_g), wa, vec(q_a_norm_g), wqmt, wqst, vec(kv_a_norm_g), wk, wvt,
            rc, rs, rct, rst, n_all_tiles, n_lat_tiles)
        atts = (_attention(qt, k, vt, 0, n_lat_tiles, 0),)
        if not last:
            atts += (_attention(qt, k, vt, n_lat_tiles, n_all_tiles - n_lat_tiles, n_lat // KC),)
        n_tiles = n_lat_tiles if last else n_all_tiles
        xa = _merge(l, xs, atts, mods, glu, cx, bg, vec(ln1_g), wg, wmo, conv_dw, vec(conv_dw_b),
                    vec(conv_ln_g), vec(conv_ln_b), wco, sc_dw, wso, wo, n_tiles, n_lat_tiles)
        xa = _ffn(l, xa, mods, vec(ln2_g), w1, w3, w2, final_g.reshape(1, -1), n_tiles,
                  n_lat_tiles, last)
        xs = (xa,)
    return xa
```

```python
import functools
import math

import jax
import jax.numpy as jnp
import numpy as np
from jax import lax
from jax.experimental import pallas as pl
from jax.experimental.pallas import tpu as pltpu

F32 = jnp.float32
BF16 = jnp.bfloat16

GRID_W = 64
MLA_HEADS = 8
QK_NOPE = 64
QK_ROPE = 32
V_HEAD = 64
Q_LORA = 384
KV_LORA = 256
CONV_CH = 256
CONV_WIDTH = 31
SC_CH = 256
SC_WIDTH = 3
N_MOD = 6
ROPE_THETA = 10000.0
EPS = 1e-6

LANES = 128
SUBLANES = 8
HEAD_PAD = LANES
TM = 256
HALO = 16
KC = 256
V_PAD = 80
M_INIT = -1e30
ATTN_UNROLL = 32
ATTN_HEADS = 2
ATTN_Q_TILES = 4
MOD_ROWS = 8
MOD_TN = 1536
VMEM_LIMIT = 56 * 1024 * 1024

Q_SCALE = (QK_NOPE + QK_ROPE) ** -0.5 * math.log2(math.e)

A_Q = 0
A_KV = A_Q + Q_LORA
A_GLU_A = A_KV + KV_LORA
A_GLU_G = A_GLU_A + CONV_CH
A_SC_B = A_GLU_G + CONV_CH
A_SC_C = A_SC_B + SC_CH
A_SC_X = A_SC_C + SC_CH
A_KR = A_SC_X + SC_CH
A_COLS = A_KR + HEAD_PAD


def _dot(a, b):
    return jnp.dot(a, b, preferred_element_type=F32)


def _rms(x, g):
    return x * lax.rsqrt(jnp.mean(x * x, axis=-1, keepdims=True) + EPS) * g


def _norm_modulate(x, g, shift, scale):
    return _rms(x, g) * (1.0 + scale) + shift


def _sigmoid(x):
    return 0.5 * jnp.tanh(0.5 * x) + 0.5


def _pad_cols(w):
    return jnp.pad(w, ((0, 0),) * (w.ndim - 1) + ((0, LANES),)).astype(BF16)


def _mod_kernel(act_ref, w_ref, b_ref, o_ref):
    a = act_ref[...]
    a = a * _sigmoid(a)
    o_ref[0] = _dot(a.astype(BF16), w_ref[0].astype(BF16)) + b_ref[0]


def _modulation(act, w_mod, b_mod):
    depth, d, n = w_mod.shape
    return pl.pallas_call(
        _mod_kernel,
        out_shape=jax.ShapeDtypeStruct((depth, MOD_ROWS, n), F32),
        grid=(depth, n // MOD_TN),
        in_specs=[
            pl.BlockSpec((MOD_ROWS, d), lambda l, j: (0, 0)),
            pl.BlockSpec((1, d, MOD_TN), lambda l, j: (l, 0, j)),
            pl.BlockSpec((1, 1, MOD_TN), lambda l, j: (l, 0, j)),
        ],
        out_specs=pl.BlockSpec((1, MOD_ROWS, MOD_TN), lambda l, j: (l, 0, j)),
        compiler_params=pltpu.CompilerParams(
            dimension_semantics=("arbitrary", "arbitrary"), vmem_limit_bytes=VMEM_LIMIT),
        name="modulation",
    )(act, w_mod, b_mod.reshape(depth, 1, n))


def _dot_t(a, b):
    return lax.dot_general(a, b, (((1,), (1,)), ((), ())), preferred_element_type=F32)


def _tile_rows(lat_ref, ctx_ref, n_lat_tiles):
    if ctx_ref is None:
        return lat_ref[0]
    return jnp.where(pl.program_id(1) >= n_lat_tiles, ctx_ref[0], lat_ref[0])


def _inproj_kernel(*refs, n_lat_tiles, split):
    x_ref, ctx_ref = (refs[0], refs[1]) if split else (refs[0], None)
    (mod_ref, lng_ref, wa_ref, qg_ref, wqmt_ref, kvg_ref, wk_ref, wvt_ref, rc_ref, rs_ref,
     rct_ref, rst_ref, qt_ref, k_ref, vt_ref, glu_ref, cx_ref, bg_ref) = refs[2 if split else 1:]
    x = _tile_rows(x_ref, ctx_ref, n_lat_tiles)
    h = _norm_modulate(x, lng_ref[...], mod_ref[0:1, :], mod_ref[1:2, :])
    proj = _dot(h.astype(BF16), wa_ref[...])
    rope = slice(QK_NOPE, QK_NOPE + QK_ROPE)
    swap = slice(QK_NOPE + QK_ROPE, HEAD_PAD)

    qn = _rms(proj[:, A_Q:A_Q + Q_LORA], qg_ref[...] * Q_SCALE).astype(BF16)
    qm = _dot_t(wqmt_ref[...], qn)
    cos_t, sin_t = rct_ref[rope, :], rst_ref[rope, :]
    pieces = []
    for hh in range(MLA_HEADS):
        head = qm[hh * HEAD_PAD:(hh + 1) * HEAD_PAD]
        pieces += [head[0:QK_NOPE], head[rope] * cos_t + head[swap] * sin_t,
                   jnp.zeros((HEAD_PAD - QK_NOPE - QK_ROPE, TM), F32)]
    qt_ref[0] = jnp.concatenate(pieces, axis=0).astype(BF16)

    cn = _rms(proj[:, A_KV:A_KV + KV_LORA], kvg_ref[...]).astype(BF16)
    kr = proj[:, A_KR:A_KR + HEAD_PAD]
    kr = kr * rc_ref[...] + pltpu.roll(kr, HEAD_PAD - QK_ROPE, axis=1) * rs_ref[...]
    k_ref[0] = (_dot(cn, wk_ref[...]) + jnp.concatenate([kr] * MLA_HEADS, axis=1)).astype(BF16)

    vt = _dot_t(wvt_ref[...], cn)
    rowid = lax.broadcasted_iota(jnp.int32, vt.shape, 0)
    vt_ref[0] = jnp.where(rowid % V_PAD == V_HEAD, 1.0, vt).astype(BF16)

    glu_ref[0] = proj[:, A_GLU_A:A_GLU_A + CONV_CH] * _sigmoid(proj[:, A_GLU_G:A_GLU_G + CONV_CH])
    bg_ref[0] = proj[:, A_SC_B:A_SC_B + SC_CH]
    cx_ref[0] = proj[:, A_SC_C:A_SC_C + SC_CH] * proj[:, A_SC_X:A_SC_X + SC_CH]


def _stream_specs(arrs, n_lat_tiles):
    width = arrs[0].shape[-1]
    if len(arrs) == 1:
        return [pl.BlockSpec((1, TM, width), lambda bi, i: (bi, i, 0))]
    assert arrs[1].shape[1] == TM
    return [pl.BlockSpec((1, TM, width), lambda bi, i: (bi, jnp.minimum(i, n_lat_tiles - 1), 0)),
            pl.BlockSpec((1, TM, width), lambda bi, i: (bi, 0, 0))]


def _layer_spec(l, arr):
    zeros = (0,) * (arr.ndim - 1)
    return pl.BlockSpec((None,) + arr.shape[1:], lambda bi, i: (l,) + zeros)


def _mod_spec(l, mods, n_lat_tiles, ctx_row):
    _, _, n_mod, d = mods.shape
    return pl.BlockSpec((None, None, n_mod, d),
                        lambda bi, i: (l, jnp.where(i < n_lat_tiles, bi, ctx_row), 0, 0))


def _inproj(l, xs, mods, lng, wa, qg, wqmt, kvg, wk, wvt, rc, rs, rct, rst, n_tiles,
            n_lat_tiles):
    b, _, d = xs[0].shape
    t_all = n_tiles * TM
    hq = MLA_HEADS * HEAD_PAD
    hv = MLA_HEADS * V_PAD
    row = lambda bi, i: (bi, i, 0)
    col = lambda bi, i: (bi, 0, i)
    params = (lng, wa, qg, wqmt, kvg, wk, wvt)
    return pl.pallas_call(
        functools.partial(_inproj_kernel, n_lat_tiles=n_lat_tiles, split=len(xs) == 2),
        out_shape=(
            jax.ShapeDtypeStruct((b, hq, t_all), BF16),
            jax.ShapeDtypeStruct((b, t_all, hq), BF16),
            jax.ShapeDtypeStruct((b, hv, t_all), BF16),
            jax.ShapeDtypeStruct((b, t_all, CONV_CH), F32),
            jax.ShapeDtypeStruct((b, t_all, SC_CH), F32),
            jax.ShapeDtypeStruct((b, t_all, SC_CH), F32),
        ),
        grid=(b, n_tiles),
        in_specs=(
            _stream_specs(xs, n_lat_tiles)
            + [_mod_spec(l, mods, n_lat_tiles, b)]
            + [_layer_spec(l, p) for p in params]
            + [pl.BlockSpec((TM, HEAD_PAD), lambda bi, i: (i, 0)),
               pl.BlockSpec((TM, HEAD_PAD), lambda bi, i: (i, 0)),
               pl.BlockSpec((HEAD_PAD, TM), lambda bi, i: (0, i)),
               pl.BlockSpec((HEAD_PAD, TM), lambda bi, i: (0, i))]),
        out_specs=(
            pl.BlockSpec((1, hq, TM), col),
            pl.BlockSpec((1, TM, hq), row),
            pl.BlockSpec((1, hv, TM), col),
            pl.BlockSpec((1, TM, CONV_CH), row),
            pl.BlockSpec((1, TM, SC_CH), row),
            pl.BlockSpec((1, TM, SC_CH), row),
        ),
        compiler_params=pltpu.CompilerParams(
            dimension_semantics=("arbitrary", "arbitrary"), vmem_limit_bytes=VMEM_LIMIT),
        name="inproj",
    )(*xs, mods, *params, rc, rs, rct, rst)


def _attn_kernel(qt_ref, k_ref, vt_ref, o_ref, *, first):
    for sub in range(qt_ref.shape[2] // TM):
        _attn_tile(qt_ref, k_ref, vt_ref, o_ref, sub * TM, first)


def _attn_tile(qt_ref, k_ref, vt_ref, o_ref, q0, first):
    n_chunks = k_ref.shape[1] // KC
    heads = range(ATTN_HEADS)
    qts = [qt_ref[0, hh * HEAD_PAD:(hh + 1) * HEAD_PAD, q0:q0 + TM] for hh in heads]

    def scores(kb):
        start = pl.multiple_of(kb * KC, KC)
        return tuple(_dot(k_ref[0, pl.ds(start, KC), hh * HEAD_PAD:(hh + 1) * HEAD_PAD], qts[hh])
                     for hh in heads)

    def softmax(sts, ms):
        out = []
        for hh in heads:
            m_new = jnp.maximum(ms[hh], jnp.max(sts[hh], axis=0, keepdims=True))
            out.append((m_new, jnp.exp2(ms[hh] - m_new), jnp.exp2(sts[hh] - m_new).astype(BF16)))
        return tuple(o[0] for o in out), tuple(o[1:] for o in out)

    def accumulate(kb, pts, accs):
        start = pl.multiple_of(jnp.maximum(kb, 0) * KC, KC)
        return tuple(
            pts[hh][0] * accs[hh] + _dot(vt_ref[0, hh * V_PAD:(hh + 1) * V_PAD, pl.ds(start, KC)],
                                         pts[hh][1])
            for hh in heads)

    def step(kb, carry):
        sts, ms, pts, accs = carry
        ms_new, pts_new = softmax(sts, ms)
        return scores(kb), ms_new, pts_new, accumulate(kb - 2, pts, accs)

    ms = tuple(jnp.full((1, TM), M_INIT, F32) for _ in heads)
    accs = tuple(jnp.zeros((V_PAD, TM), F32) for _ in heads)
    pts = tuple((jnp.ones((1, TM), F32), jnp.zeros((KC, TM), BF16)) for _ in heads)
    carry = (scores(first), ms, pts, accs)
    if n_chunks - 1 - first:
        carry = lax.fori_loop(first + 1, n_chunks, step, carry, unroll=ATTN_UNROLL)
    sts, ms, pts, accs = carry
    accs = accumulate(n_chunks - 2, pts, accs)
    _, pts = softmax(sts, ms)
    accs = accumulate(n_chunks - 1, pts, accs)
    outs = [(acc[0:V_HEAD] / acc[V_HEAD:V_HEAD + 1]).T for acc in accs]
    o_ref[0, q0:q0 + TM, :] = jnp.concatenate(outs, axis=1).astype(BF16)


def _attention(qt, k, vt, q_tile_lo, n_q_tiles, first_chunk):
    b, t_all, _ = k.shape
    groups = MLA_HEADS // ATTN_HEADS
    per = ATTN_Q_TILES if n_q_tiles % ATTN_Q_TILES == 0 and q_tile_lo % ATTN_Q_TILES == 0 else 1
    tq = per * TM
    lo = q_tile_lo // per
    return pl.pallas_call(
        functools.partial(_attn_kernel, first=first_chunk),
        out_shape=jax.ShapeDtypeStruct((b, n_q_tiles * TM, MLA_HEADS * V_HEAD), BF16),
        grid=(b, groups, n_q_tiles // per),
        in_specs=[
            pl.BlockSpec((1, ATTN_HEADS * HEAD_PAD, tq), lambda bi, j, i: (bi, j, i + lo)),
            pl.BlockSpec((1, t_all, ATTN_HEADS * HEAD_PAD), lambda bi, j, i: (bi, 0, j)),
            pl.BlockSpec((1, ATTN_HEADS * V_PAD, t_all), lambda bi, j, i: (bi, j, 0)),
        ],
        out_specs=pl.BlockSpec((1, tq, ATTN_HEADS * V_HEAD), lambda bi, j, i: (bi, i, j)),
        compiler_params=pltpu.CompilerParams(
            dimension_semantics=("arbitrary", "arbitrary", "arbitrary"), vmem_limit_bytes=VMEM_LIMIT),
        name="attention",
    )(qt, k, vt)


def _fill_halo_buffer(buf, cur_ref, prev_ref, next_ref, keep_prev, keep_next):
    buf[0:HALO, :] = prev_ref[0] * keep_prev
    buf[HALO:HALO + TM, :] = cur_ref[0]
    buf[HALO + TM:HALO + TM + HALO, :] = next_ref[0] * keep_next


def _depthwise(buf, w_ref, width):
    base = HALO - (width - 1) // 2
    out = None
    for r in range(SUBLANES):
        acc = None
        for k in range(width):
            if (base + k) % SUBLANES != r:
                continue
            off = base + k - r
            term = w_ref[k:k + 1, :] * buf[off:off + TM + SUBLANES, :]
            acc = term if acc is None else acc + term
        if acc is not None:
            out = acc[r:r + TM, :] if out is None else out + acc[r:r + TM, :]
    return out


def _merge_kernel(*refs, n_lat_tiles, x_split, att_split):
    refs = list(refs)
    x_ref = refs.pop(0)
    xc_ref = refs.pop(0) if x_split else None
    att_ref = refs.pop(0)
    attc_ref = refs.pop(0) if att_split else None
    (mod_ref, glu_ref, glu_p_ref, glu_n_ref, cx_ref, cx_p_ref, cx_n_ref, bg_ref, lng_ref, wg_ref,
     wmo_ref, dw_ref, dwb_ref, cg_ref, cb_ref, wco_ref, scw_ref, wso_ref, wo_ref, o_ref, cbuf,
     sbuf) = refs
    i = pl.program_id(1)
    is_ctx = i >= n_lat_tiles
    keep_prev = jnp.where((i == 0) | is_ctx, 0.0, 1.0)
    keep_next = jnp.where((i == n_lat_tiles - 1) | is_ctx, 0.0, 1.0)
    _fill_halo_buffer(cbuf, glu_ref, glu_p_ref, glu_n_ref, keep_prev, keep_next)
    _fill_halo_buffer(sbuf, cx_ref, cx_p_ref, cx_n_ref, keep_prev, keep_next)

    x = _tile_rows(x_ref, xc_ref, n_lat_tiles)
    d = x.shape[-1]
    h = _norm_modulate(x, lng_ref[...], mod_ref[0:1, :], mod_ref[1:2, :]).astype(BF16)

    def gate(branch):
        return _sigmoid(_dot(h, wg_ref[:, branch * d:(branch + 1) * d]))

    merged = gate(0) * _dot(_tile_rows(att_ref, attc_ref, n_lat_tiles), wmo_ref[:, 0:d])

    u = _depthwise(cbuf, dw_ref, CONV_WIDTH) + dwb_ref[...]
    mu = jnp.mean(u, axis=-1, keepdims=True)
    uc = u - mu
    var = jnp.mean(uc * uc, axis=-1, keepdims=True)
    u = uc * lax.rsqrt(var + EPS) * cg_ref[...] + cb_ref[...]
    u = u * _sigmoid(u)
    merged = merged + gate(1) * _dot(u.astype(BF16), wco_ref[:, 0:d])

    sc = bg_ref[0] * _depthwise(sbuf, scw_ref, SC_WIDTH)
    merged = merged + gate(2) * _dot(sc.astype(BF16), wso_ref[:, 0:d])

    o_ref[0] = x + mod_ref[2:3, :] * _dot(merged.astype(BF16), wo_ref[:, 0:d])


def _merge(l, xs, atts, mods, glu, cx, bg, lng, wg, wmo, dw, dwb, cg, cb, wco, scw, wso, wo, n_tiles,
           n_lat_tiles):
    b, _, d = xs[0].shape
    per = TM // HALO
    last_halo = glu.shape[1] // HALO - 1
    row = lambda bi, i: (bi, i, 0)
    prev = lambda bi, i: (bi, jnp.maximum(i * per - 1, 0), 0)
    nxt = lambda bi, i: (bi, jnp.minimum((i + 1) * per, last_halo), 0)
    params = (lng, wg, wmo, dw, dwb, cg, cb, wco, scw, wso, wo)
    return pl.pallas_call(
        functools.partial(_merge_kernel, n_lat_tiles=n_lat_tiles, x_split=len(xs) == 2,
                          att_split=len(atts) == 2),
        out_shape=jax.ShapeDtypeStruct((b, n_tiles * TM, d), F32),
        grid=(b, n_tiles),
        in_specs=(
            _stream_specs(xs, n_lat_tiles)
            + _stream_specs(atts, n_lat_tiles)
            + [_mod_spec(l, mods, n_lat_tiles, b),
               pl.BlockSpec((1, TM, CONV_CH), row),
               pl.BlockSpec((1, HALO, CONV_CH), prev),
               pl.BlockSpec((1, HALO, CONV_CH), nxt),
               pl.BlockSpec((1, TM, SC_CH), row),
               pl.BlockSpec((1, HALO, SC_CH), prev),
               pl.BlockSpec((1, HALO, SC_CH), nxt),
               pl.BlockSpec((1, TM, SC_CH), row)]
            + [_layer_spec(l, p) for p in params]),
        out_specs=pl.BlockSpec((1, TM, d), row),
        scratch_shapes=[
            pltpu.VMEM((TM + 2 * HALO, CONV_CH), F32),
            pltpu.VMEM((TM + 2 * HALO, SC_CH), F32),
        ],
        compiler_params=pltpu.CompilerParams(
            dimension_semantics=("arbitrary", "arbitrary"), vmem_limit_bytes=VMEM_LIMIT),
        name="merge",
    )(*xs, *atts, mods, glu, glu, glu, cx, cx, cx, bg, *params)


def _ffn_kernel(x_ref, mod_ref, lng_ref, w1_ref, w3_ref, w2_ref, fg_ref, o_ref, *, final_norm):
    x = x_ref[0]
    h = _norm_modulate(x, lng_ref[...], mod_ref[3:4, :], mod_ref[4:5, :]).astype(BF16)
    a = _dot(h, w1_ref[...])
    a = a * _sigmoid(a) * _dot(h, w3_ref[...])
    y = x + mod_ref[5:6, :] * _dot(a.astype(BF16), w2_ref[...])
    if final_norm:
        y = _rms(y, fg_ref[...])
    o_ref[0] = y


def _ffn(l, xa, mods, lng, w1, w3, w2, fg, n_tiles, n_lat_tiles, final_norm):
    b, _, d = xa.shape
    row = lambda bi, i: (bi, i, 0)
    return pl.pallas_call(
        functools.partial(_ffn_kernel, final_norm=final_norm),
        out_shape=jax.ShapeDtypeStruct((b, n_tiles * TM, d), F32),
        grid=(b, n_tiles),
        in_specs=[
            pl.BlockSpec((1, TM, d), row),
            _mod_spec(l, mods, n_lat_tiles, b),
            _layer_spec(l, lng),
            _layer_spec(l, w1),
            _layer_spec(l, w3),
            _layer_spec(l, w2),
            pl.BlockSpec((1, d), lambda bi, i: (0, 0)),
        ],
        out_specs=pl.BlockSpec((1, TM, d), row),
        compiler_params=pltpu.CompilerParams(
            dimension_semantics=("arbitrary", "arbitrary"), vmem_limit_bytes=VMEM_LIMIT),
        name="ffn",
    )(xa, mods, lng, w1, w3, w2, fg)


def _rope_patterns(n_lat, n_ctx):
    t = np.arange(n_lat)
    half = QK_ROPE // 2
    inv = np.float64(ROPE_THETA) ** (-np.arange(0, half, 2, dtype=np.float64) / half)
    ang_r = (t // GRID_W)[:, None] * inv
    ang_c = (t % GRID_W)[:, None] * inv
    cos = np.concatenate([np.cos(ang_r)] * 2 + [np.cos(ang_c)] * 2, axis=1)
    sin = np.concatenate([-np.sin(ang_r), np.sin(ang_r), -np.sin(ang_c), np.sin(ang_c)], axis=1)
    pad = HEAD_PAD - QK_NOPE - QK_ROPE
    rc = np.concatenate([np.ones((n_lat, QK_NOPE)), cos, np.zeros((n_lat, pad))], axis=1)
    rs = np.concatenate([np.zeros((n_lat, QK_NOPE)), sin, np.zeros((n_lat, pad))], axis=1)
    rc_ctx = np.concatenate([np.ones((n_ctx, QK_NOPE + QK_ROPE)), np.zeros((n_ctx, pad))], axis=1)
    rc = np.concatenate([rc, rc_ctx], axis=0).astype(np.float32)
    rs = np.concatenate([rs, np.zeros((n_ctx, HEAD_PAD))], axis=0).astype(np.float32)
    return rc, rs, np.ascontiguousarray(rc.T), np.ascontiguousarray(rs.T)


def _swap_rope_columns(w):
    idx = jnp.arange(QK_ROPE) ^ (QK_ROPE // 4)
    return w[..., idx]


def _projection_weights(w_in, w_q_b, w_kv_b):
    depth, d, _ = w_in.shape
    q_end = Q_LORA
    kv_end = q_end + KV_LORA + QK_ROPE
    sc_end = kv_end + 2 * CONV_CH + 3 * SC_CH
    w_kr = w_in[..., q_end + KV_LORA:kv_end]
    assert HEAD_PAD == QK_NOPE + 2 * QK_ROPE
    pad_lo = jnp.zeros((depth, d, QK_NOPE), w_in.dtype)
    wa = jnp.concatenate([
        w_in[..., :q_end + KV_LORA], w_in[..., kv_end:sc_end],
        pad_lo, w_kr, _swap_rope_columns(w_kr)], axis=-1).astype(BF16)
    wg = _pad_cols(w_in[..., sc_end:])

    wq = w_q_b.reshape(depth, Q_LORA, MLA_HEADS, QK_NOPE + QK_ROPE)
    wqm = jnp.concatenate([wq, _swap_rope_columns(wq[..., QK_NOPE:])], axis=-1)
    wqm = wqm.reshape(depth, Q_LORA, MLA_HEADS * HEAD_PAD)

    wkv = w_kv_b.reshape(depth, KV_LORA, MLA_HEADS, QK_NOPE + V_HEAD)
    wk = jnp.concatenate(
        [wkv[..., :QK_NOPE],
         jnp.zeros((depth, KV_LORA, MLA_HEADS, HEAD_PAD - QK_NOPE), w_kv_b.dtype)], axis=-1)
    wk = wk.reshape(depth, KV_LORA, MLA_HEADS * HEAD_PAD).astype(BF16)
    wv = jnp.concatenate(
        [wkv[..., QK_NOPE:],
         jnp.zeros((depth, KV_LORA, MLA_HEADS, V_PAD - V_HEAD), w_kv_b.dtype)], axis=-1)
    wvt = jnp.swapaxes(wv.reshape(depth, KV_LORA, MLA_HEADS * V_PAD), 1, 2).astype(BF16)
    return wa, wg, jnp.swapaxes(wqm, 1, 2).astype(BF16), wk, wvt


def kernel(x, c, ctx, c_ctx, w_mod, b_mod, ln1_g, w_in, q_a_norm_g, w_q_b, kv_a_norm_g, w_kv_b, w_mla_o, conv_dw, conv_dw_b, conv_ln_g, conv_ln_b, w_conv_o, sc_dw, w_sc_o, w_o, ln2_g, w_ff1, w_ff3, w_ff2, final_g):
    b, n_lat, d = x.shape
    n_ctx = ctx.shape[1]
    depth = w_mod.shape[0]
    assert n_ctx == TM and n_lat % TM == 0 and TM % KC == 0 and b + 1 <= MOD_ROWS
    n_lat_tiles = n_lat // TM
    n_all_tiles = (n_lat + n_ctx) // TM

    act = jnp.concatenate([c, c_ctx[None, :], jnp.zeros((MOD_ROWS - b - 1, d), F32)], axis=0)
    mods = _modulation(act, w_mod, b_mod).reshape(depth, MOD_ROWS, N_MOD, d)

    rc, rs, rct, rst = _rope_patterns(n_lat, n_ctx)
    vec = lambda v: v.reshape(depth, 1, -1)
    wa, wg, wqmt, wk, wvt = _projection_weights(w_in, w_q_b, w_kv_b)
    wmo, wco, wso, wo = (_pad_cols(w) for w in (w_mla_o, w_conv_o, w_sc_o, w_o))
    w1, w3, w2 = (w.astype(BF16) for w in (w_ff1, w_ff3, w_ff2))

    xs = (x, ctx)
    for l in range(depth):
        last = l == depth - 1
        qt, k, vt, glu, cx, bg = _inproj(
            l, xs, mods, vec(ln1_g), wa, vec(q_a_norm_g), wqmt, vec(kv_a_norm_g), wk, wvt,
            rc, rs, rct, rst, n_all_tiles, n_lat_tiles)
        atts = (_attention(qt, k, vt, 0, n_lat_tiles, 0),)
        if not last:
            atts += (_attention(qt, k, vt, n_lat_tiles, n_all_tiles - n_lat_tiles, n_lat // KC),)
        n_tiles = n_lat_tiles if last else n_all_tiles
        xa = _merge(l, xs, atts, mods, glu, cx, bg, vec(ln1_g), wg, wmo, conv_dw, vec(conv_dw_b),
                    vec(conv_ln_g), vec(conv_ln_b), wco, sc_dw, wso, wo, n_tiles, n_lat_tiles)
        xa = _ffn(l, xa, mods, vec(ln2_g), w1, w3, w2, final_g.reshape(1, -1), n_tiles,
                  n_lat_tiles, last)
        xs = (xa,)
    return xa
```

```python
import functools
import math

import jax
import jax.numpy as jnp
import numpy as np
from jax import lax
from jax.experimental import pallas as pl
from jax.experimental.pallas import tpu as pltpu

F32 = jnp.float32
BF16 = jnp.bfloat16

GRID_W = 64
MLA_HEADS = 8
QK_NOPE = 64
QK_ROPE = 32
V_HEAD = 64
Q_LORA = 384
KV_LORA = 256
CONV_CH = 256
CONV_WIDTH = 31
SC_CH = 256
SC_WIDTH = 3
N_MOD = 6
ROPE_THETA = 10000.0
EPS = 1e-6

LANES = 128
SUBLANES = 8
HEAD_PAD = LANES
TM = 256
HALO = 16
KC = 256
V_PAD = 80
M_INIT = -1e30
ATTN_HEADS = 2
ATTN_Q_TILES = 4
FFN_TILES = 3
MOD_ROWS = 8
MOD_TN = 1536
VMEM_LIMIT = 56 * 1024 * 1024

Q_SCALE = (QK_NOPE + QK_ROPE) ** -0.5 * math.log2(math.e)

A_Q = 0
A_KV = A_Q + Q_LORA
A_GLU_A = A_KV + KV_LORA
A_GLU_G = A_GLU_A + CONV_CH
A_SC_B = A_GLU_G + CONV_CH
A_SC_C = A_SC_B + SC_CH
A_SC_X = A_SC_C + SC_CH
A_KR = A_SC_X + SC_CH
A_COLS = A_KR + HEAD_PAD


def _dot(a, b):
    return jnp.dot(a, b, preferred_element_type=F32)


def _rms(x, g):
    return x * lax.rsqrt(jnp.mean(x * x, axis=-1, keepdims=True) + EPS) * g


def _norm_modulate(x, g, shift, scale):
    return _rms(x, g) * (1.0 + scale) + shift


def _sigmoid(x):
    return 0.5 * jnp.tanh(0.5 * x) + 0.5


def _pad_cols(w):
    return jnp.pad(w, ((0, 0),) * (w.ndim - 1) + ((0, LANES),)).astype(BF16)


def _mod_kernel(act_ref, w_ref, b_ref, o_ref):
    a = act_ref[...]
    a = a * _sigmoid(a)
    o_ref[0] = _dot(a.astype(BF16), w_ref[0].astype(BF16)) + b_ref[0]


def _modulation(act, w_mod, b_mod):
    depth, d, n = w_mod.shape
    return pl.pallas_call(
        _mod_kernel,
        out_shape=jax.ShapeDtypeStruct((depth, MOD_ROWS, n), F32),
        grid=(depth, n // MOD_TN),
        in_specs=[
            pl.BlockSpec((MOD_ROWS, d), lambda l, j: (0, 0)),
            pl.BlockSpec((1, d, MOD_TN), lambda l, j: (l, 0, j)),
            pl.BlockSpec((1, 1, MOD_TN), lambda l, j: (l, 0, j)),
        ],
        out_specs=pl.BlockSpec((1, MOD_ROWS, MOD_TN), lambda l, j: (l, 0, j)),
        compiler_params=pltpu.CompilerParams(
            dimension_semantics=("arbitrary", "arbitrary"), vmem_limit_bytes=VMEM_LIMIT),
        name="modulation",
    )(act, w_mod, b_mod.reshape(depth, 1, n))


def _dot_t(a, b):
    return lax.dot_general(a, b, (((1,), (1,)), ((), ())), preferred_element_type=F32)


def _tile_rows(lat_ref, ctx_ref, n_lat_tiles):
    if ctx_ref is None:
        return lat_ref[0]
    return jnp.where(pl.program_id(1) >= n_lat_tiles, ctx_ref[0], lat_ref[0])


def _inproj_kernel(*refs, n_lat_tiles, split):
    x_ref, ctx_ref = (refs[0], refs[1]) if split else (refs[0], None)
    (mod_ref, lng_ref, wa_ref, qg_ref, wqmt_ref, kvg_ref, wk_ref, wvt_ref, rc_ref, rs_ref,
     rct_ref, rst_ref, qt_ref, k_ref, vt_ref, glu_ref, cx_ref, bg_ref) = refs[2 if split else 1:]
    x = _tile_rows(x_ref, ctx_ref, n_lat_tiles)
    h = _norm_modulate(x, lng_ref[...], mod_ref[0:1, :], mod_ref[1:2, :])
    proj = _dot(h.astype(BF16), wa_ref[...])
    rope = slice(QK_NOPE, QK_NOPE + QK_ROPE)
    swap = slice(QK_NOPE + QK_ROPE, HEAD_PAD)

    qn = _rms(proj[:, A_Q:A_Q + Q_LORA], qg_ref[...] * Q_SCALE).astype(BF16)
    qm = _dot_t(wqmt_ref[...], qn)
    cos_t, sin_t = rct_ref[rope, :], rst_ref[rope, :]
    pieces = []
    for hh in range(MLA_HEADS):
        head = qm[hh * HEAD_PAD:(hh + 1) * HEAD_PAD]
        pieces += [head[0:QK_NOPE], head[rope] * cos_t + head[swap] * sin_t,
                   jnp.zeros((HEAD_PAD - QK_NOPE - QK_ROPE, TM), F32)]
    qt_ref[0] = jnp.concatenate(pieces, axis=0).astype(BF16)

    cn = _rms(proj[:, A_KV:A_KV + KV_LORA], kvg_ref[...]).astype(BF16)
    kr = proj[:, A_KR:A_KR + HEAD_PAD]
    kr = kr * rc_ref[...] + pltpu.roll(kr, HEAD_PAD - QK_ROPE, axis=1) * rs_ref[...]
    k_ref[0] = (_dot(cn, wk_ref[...]) + jnp.concatenate([kr] * MLA_HEADS, axis=1)).astype(BF16)

    vt = _dot_t(wvt_ref[...], cn)
    rowid = lax.broadcasted_iota(jnp.int32, vt.shape, 0)
    vt_ref[0] = jnp.where(rowid % V_PAD == V_HEAD, 1.0, vt).astype(BF16)

    glu_ref[0] = proj[:, A_GLU_A:A_GLU_A + CONV_CH] * _sigmoid(proj[:, A_GLU_G:A_GLU_G + CONV_CH])
    bg_ref[0] = proj[:, A_SC_B:A_SC_B + SC_CH]
    cx_ref[0] = proj[:, A_SC_C:A_SC_C + SC_CH] * proj[:, A_SC_X:A_SC_X + SC_CH]


def _stream_specs(arrs, n_lat_tiles):
    width = arrs[0].shape[-1]
    if len(arrs) == 1:
        return [pl.BlockSpec((1, TM, width), lambda bi, i: (bi, i, 0))]
    assert arrs[1].shape[1] == TM
    return [pl.BlockSpec((1, TM, width), lambda bi, i: (bi, jnp.minimum(i, n_lat_tiles - 1), 0)),
            pl.BlockSpec((1, TM, width), lambda bi, i: (bi, 0, 0))]


def _layer_spec(l, arr):
    zeros = (0,) * (arr.ndim - 1)
    return pl.BlockSpec((None,) + arr.shape[1:], lambda bi, i: (l,) + zeros)


def _mod_spec(l, mods, n_lat_tiles, ctx_row):
    _, _, n_mod, d = mods.shape
    return pl.BlockSpec((None, None, n_mod, d),
                        lambda bi, i: (l, jnp.where(i < n_lat_tiles, bi, ctx_row), 0, 0))


def _inproj(l, xs, mods, lng, wa, qg, wqmt, kvg, wk, wvt, rc, rs, rct, rst, n_tiles,
            n_lat_tiles):
    b, _, d = xs[0].shape
    t_all = n_tiles * TM
    hq = MLA_HEADS * HEAD_PAD
    hv = MLA_HEADS * V_PAD
    row = lambda bi, i: (bi, i, 0)
    col = lambda bi, i: (bi, 0, i)
    params = (lng, wa, qg, wqmt, kvg, wk, wvt)
    return pl.pallas_call(
        functools.partial(_inproj_kernel, n_lat_tiles=n_lat_tiles, split=len(xs) == 2),
        out_shape=(
            jax.ShapeDtypeStruct((b, hq, t_all), BF16),
            jax.ShapeDtypeStruct((b, t_all, hq), BF16),
            jax.ShapeDtypeStruct((b, hv, t_all), BF16),
            jax.ShapeDtypeStruct((b, t_all, CONV_CH), F32),
            jax.ShapeDtypeStruct((b, t_all, SC_CH), F32),
            jax.ShapeDtypeStruct((b, t_all, SC_CH), F32),
        ),
        grid=(b, n_tiles),
        in_specs=(
            _stream_specs(xs, n_lat_tiles)
            + [_mod_spec(l, mods, n_lat_tiles, b)]
            + [_layer_spec(l, p) for p in params]
            + [pl.BlockSpec((TM, HEAD_PAD), lambda bi, i: (i, 0)),
               pl.BlockSpec((TM, HEAD_PAD), lambda bi, i: (i, 0)),
               pl.BlockSpec((HEAD_PAD, TM), lambda bi, i: (0, i)),
               pl.BlockSpec((HEAD_PAD, TM), lambda bi, i: (0, i))]),
        out_specs=(
            pl.BlockSpec((1, hq, TM), col),
            pl.BlockSpec((1, TM, hq), row),
            pl.BlockSpec((1, hv, TM), col),
            pl.BlockSpec((1, TM, CONV_CH), row),
            pl.BlockSpec((1, TM, SC_CH), row),
            pl.BlockSpec((1, TM, SC_CH), row),
        ),
        compiler_params=pltpu.CompilerParams(
            dimension_semantics=("arbitrary", "arbitrary"), vmem_limit_bytes=VMEM_LIMIT),
        name="inproj",
    )(*xs, mods, *params, rc, rs, rct, rst)


def _attn_kernel(qt_ref, k_ref, vt_ref, o_ref):
    for sub in range(qt_ref.shape[2] // TM):
        _attn_tile(qt_ref, k_ref, vt_ref, o_ref, sub * TM)


def _attn_tile(qt_ref, k_ref, vt_ref, o_ref, q0):
    n_keys = k_ref.shape[1]
    chunks = [(s, min(KC, n_keys - s)) for s in range(0, n_keys, KC)]
    heads = range(ATTN_HEADS)
    qts = [qt_ref[0, hh * HEAD_PAD:(hh + 1) * HEAD_PAD, q0:q0 + TM] for hh in heads]

    def scores(chunk):
        s0, n = chunk
        return tuple(_dot(k_ref[0, s0:s0 + n, hh * HEAD_PAD:(hh + 1) * HEAD_PAD], qts[hh])
                     for hh in heads)

    def softmax(sts, ms):
        out = []
        for hh in heads:
            m_new = jnp.maximum(ms[hh], jnp.max(sts[hh], axis=0, keepdims=True))
            out.append((m_new, jnp.exp2(ms[hh] - m_new), jnp.exp2(sts[hh] - m_new).astype(BF16)))
        return tuple(o[0] for o in out), tuple(o[1:] for o in out)

    def accumulate(chunk, pts, accs):
        s0, n = chunk
        return tuple(
            pts[hh][0] * accs[hh] + _dot(vt_ref[0, hh * V_PAD:(hh + 1) * V_PAD, s0:s0 + n],
                                         pts[hh][1])
            for hh in heads)

    ms = tuple(jnp.full((1, TM), M_INIT, F32) for _ in heads)
    accs = tuple(jnp.zeros((V_PAD, TM), F32) for _ in heads)
    sts, pts = scores(chunks[0]), None
    for t in range(1, len(chunks) + 2):
        new_pts = None
        if t - 1 < len(chunks):
            ms, new_pts = softmax(sts, ms)
        sts = scores(chunks[t]) if t < len(chunks) else None
        if pts is not None:
            accs = accumulate(chunks[t - 2], pts, accs)
        pts = new_pts
    outs = [(acc[0:V_HEAD] / acc[V_HEAD:V_HEAD + 1]).T for acc in accs]
    o_ref[0, q0:q0 + TM, :] = jnp.concatenate(outs, axis=1).astype(BF16)


def _attention(qt, k, vt, q_tile_lo, n_q_tiles, first_key):
    b, t_all, _ = k.shape
    groups = MLA_HEADS // ATTN_HEADS
    per = ATTN_Q_TILES if n_q_tiles % ATTN_Q_TILES == 0 and q_tile_lo % ATTN_Q_TILES == 0 else 1
    tq = per * TM
    lo = q_tile_lo // per
    n_keys = t_all - first_key
    assert first_key % n_keys == 0
    key_blk = first_key // n_keys
    return pl.pallas_call(
        _attn_kernel,
        out_shape=jax.ShapeDtypeStruct((b, n_q_tiles * TM, MLA_HEADS * V_HEAD), BF16),
        grid=(b, groups, n_q_tiles // per),
        in_specs=[
            pl.BlockSpec((1, ATTN_HEADS * HEAD_PAD, tq), lambda bi, j, i: (bi, j, i + lo)),
            pl.BlockSpec((1, n_keys, ATTN_HEADS * HEAD_PAD), lambda bi, j, i: (bi, key_blk, j)),
            pl.BlockSpec((1, ATTN_HEADS * V_PAD, n_keys), lambda bi, j, i: (bi, j, key_blk)),
        ],
        out_specs=pl.BlockSpec((1, tq, ATTN_HEADS * V_HEAD), lambda bi, j, i: (bi, i, j)),
        compiler_params=pltpu.CompilerParams(
            dimension_semantics=("arbitrary", "arbitrary", "arbitrary"), vmem_limit_bytes=VMEM_LIMIT),
        name="attention",
    )(qt, k, vt)


def _fill_halo_buffer(buf, cur_ref, prev_ref, next_ref, keep_prev, keep_next):
    buf[0:HALO, :] = prev_ref[0] * keep_prev
    buf[HALO:HALO + TM, :] = cur_ref[0]
    buf[HALO + TM:HALO + TM + HALO, :] = next_ref[0] * keep_next


def _depthwise(buf, w_ref, width):
    base = HALO - (width - 1) // 2
    out = None
    for r in range(SUBLANES):
        acc = None
        for k in range(width):
            if (base + k) % SUBLANES != r:
                continue
            off = base + k - r
            term = w_ref[k:k + 1, :] * buf[off:off + TM + SUBLANES, :]
            acc = term if acc is None else acc + term
        if acc is not None:
            out = acc[r:r + TM, :] if out is None else out + acc[r:r + TM, :]
    return out


def _merge_kernel(*refs, n_lat_tiles, x_split, att_split):
    refs = list(refs)
    x_ref = refs.pop(0)
    xc_ref = refs.pop(0) if x_split else None
    att_ref = refs.pop(0)
    attc_ref = refs.pop(0) if att_split else None
    (mod_ref, glu_ref, glu_p_ref, glu_n_ref, cx_ref, cx_p_ref, cx_n_ref, bg_ref, lng_ref, wg_ref,
     wmo_ref, dw_ref, dwb_ref, cg_ref, cb_ref, wco_ref, scw_ref, wso_ref, wo_ref, o_ref, cbuf,
     sbuf) = refs
    i = pl.program_id(1)
    is_ctx = i >= n_lat_tiles
    keep_prev = jnp.where((i == 0) | is_ctx, 0.0, 1.0)
    keep_next = jnp.where((i == n_lat_tiles - 1) | is_ctx, 0.0, 1.0)
    _fill_halo_buffer(cbuf, glu_ref, glu_p_ref, glu_n_ref, keep_prev, keep_next)
    _fill_halo_buffer(sbuf, cx_ref, cx_p_ref, cx_n_ref, keep_prev, keep_next)

    x = _tile_rows(x_ref, xc_ref, n_lat_tiles)
    d = x.shape[-1]
    h = _norm_modulate(x, lng_ref[...], mod_ref[0:1, :], mod_ref[1:2, :]).astype(BF16)

    def gate(branch):
        return _sigmoid(_dot(h, wg_ref[:, branch * d:(branch + 1) * d]))

    merged = gate(0) * _dot(_tile_rows(att_ref, attc_ref, n_lat_tiles), wmo_ref[:, 0:d])

    u = _depthwise(cbuf, dw_ref, CONV_WIDTH) + dwb_ref[...]
    mu = jnp.mean(u, axis=-1, keepdims=True)
    uc = u - mu
    var = jnp.mean(uc * uc, axis=-1, keepdims=True)
    u = uc * lax.rsqrt(var + EPS) * cg_ref[...] + cb_ref[...]
    u = u * _sigmoid(u)
    merged = merged + gate(1) * _dot(u.astype(BF16), wco_ref[:, 0:d])

    sc = bg_ref[0] * _depthwise(sbuf, scw_ref, SC_WIDTH)
    merged = merged + gate(2) * _dot(sc.astype(BF16), wso_ref[:, 0:d])

    o_ref[0] = x + mod_ref[2:3, :] * _dot(merged.astype(BF16), wo_ref[:, 0:d])


def _merge(l, xs, atts, mods, glu, cx, bg, lng, wg, wmo, dw, dwb, cg, cb, wco, scw, wso, wo, n_tiles,
           n_lat_tiles):
    b, _, d = xs[0].shape
    per = TM // HALO
    last_halo = glu.shape[1] // HALO - 1
    row = lambda bi, i: (bi, i, 0)
    prev = lambda bi, i: (bi, jnp.maximum(i * per - 1, 0), 0)
    nxt = lambda bi, i: (bi, jnp.minimum((i + 1) * per, last_halo), 0)
    params = (lng, wg, wmo, dw, dwb, cg, cb, wco, scw, wso, wo)
    return pl.pallas_call(
        functools.partial(_merge_kernel, n_lat_tiles=n_lat_tiles, x_split=len(xs) == 2,
                          att_split=len(atts) == 2),
        out_shape=jax.ShapeDtypeStruct((b, n_tiles * TM, d), F32),
        grid=(b, n_tiles),
        in_specs=(
            _stream_specs(xs, n_lat_tiles)
            + _stream_specs(atts, n_lat_tiles)
            + [_mod_spec(l, mods, n_lat_tiles, b),
               pl.BlockSpec((1, TM, CONV_CH), row),
               pl.BlockSpec((1, HALO, CONV_CH), prev),
               pl.BlockSpec((1, HALO, CONV_CH), nxt),
               pl.BlockSpec((1, TM, SC_CH), row),
               pl.BlockSpec((1, HALO, SC_CH), prev),
               pl.BlockSpec((1, HALO, SC_CH), nxt),
               pl.BlockSpec((1, TM, SC_CH), row)]
            + [_layer_spec(l, p) for p in params]),
        out_specs=pl.BlockSpec((1, TM, d), row),
        scratch_shapes=[
            pltpu.VMEM((TM + 2 * HALO, CONV_CH), F32),
            pltpu.VMEM((TM + 2 * HALO, SC_CH), F32),
        ],
        compiler_params=pltpu.CompilerParams(
            dimension_semantics=("arbitrary", "arbitrary"), vmem_limit_bytes=VMEM_LIMIT),
        name="merge",
    )(*xs, *atts, mods, glu, glu, glu, cx, cx, cx, bg, *params)


def _ffn_kernel(x_ref, mod_ref, modc_ref, lng_ref, w1_ref, w3_ref, w2_ref, fg_ref, o_ref, *,
                n_lat_tiles, final_norm):
    per = x_ref.shape[1] // TM
    for sub in range(per):
        rows = slice(sub * TM, (sub + 1) * TM)
        is_ctx = pl.program_id(1) * per + sub >= n_lat_tiles
        mod = jnp.where(is_ctx, modc_ref[...], mod_ref[...])
        x = x_ref[0, rows, :]
        h = _norm_modulate(x, lng_ref[...], mod[3:4, :], mod[4:5, :]).astype(BF16)
        a = _dot(h, w1_ref[...])
        a = a * _sigmoid(a) * _dot(h, w3_ref[...])
        y = x + mod[5:6, :] * _dot(a.astype(BF16), w2_ref[...])
        if final_norm:
            y = _rms(y, fg_ref[...])
        o_ref[0, rows, :] = y


def _ffn(l, xa, mods, lng, w1, w3, w2, fg, n_tiles, n_lat_tiles, final_norm):
    b, _, d = xa.shape
    per = max(p for p in range(1, FFN_TILES + 1) if n_tiles % p == 0)
    row = lambda bi, i: (bi, i, 0)
    n_mod = mods.shape[2]
    return pl.pallas_call(
        functools.partial(_ffn_kernel, n_lat_tiles=n_lat_tiles, final_norm=final_norm),
        out_shape=jax.ShapeDtypeStruct((b, n_tiles * TM, d), F32),
        grid=(b, n_tiles // per),
        in_specs=[
            pl.BlockSpec((1, per * TM, d), row),
            pl.BlockSpec((None, None, n_mod, d), lambda bi, i: (l, bi, 0, 0)),
            pl.BlockSpec((None, None, n_mod, d), lambda bi, i: (l, b, 0, 0)),
            _layer_spec(l, lng),
            _layer_spec(l, w1),
            _layer_spec(l, w3),
            _layer_spec(l, w2),
            pl.BlockSpec((1, d), lambda bi, i: (0, 0)),
        ],
        out_specs=pl.BlockSpec((1, per * TM, d), row),
        compiler_params=pltpu.CompilerParams(
            dimension_semantics=("arbitrary", "arbitrary"), vmem_limit_bytes=VMEM_LIMIT),
        name="ffn",
    )(xa, mods, mods, lng, w1, w3, w2, fg)


def _rope_patterns(n_lat, n_ctx):
    t = np.arange(n_lat)
    half = QK_ROPE // 2
    inv = np.float64(ROPE_THETA) ** (-np.arange(0, half, 2, dtype=np.float64) / half)
    ang_r = (t // GRID_W)[:, None] * inv
    ang_c = (t % GRID_W)[:, None] * inv
    cos = np.concatenate([np.cos(ang_r)] * 2 + [np.cos(ang_c)] * 2, axis=1)
    sin = np.concatenate([-np.sin(ang_r), np.sin(ang_r), -np.sin(ang_c), np.sin(ang_c)], axis=1)
    pad = HEAD_PAD - QK_NOPE - QK_ROPE
    rc = np.concatenate([np.ones((n_lat, QK_NOPE)), cos, np.zeros((n_lat, pad))], axis=1)
    rs = np.concatenate([np.zeros((n_lat, QK_NOPE)), sin, np.zeros((n_lat, pad))], axis=1)
    rc_ctx = np.concatenate([np.ones((n_ctx, QK_NOPE + QK_ROPE)), np.zeros((n_ctx, pad))], axis=1)
    rc = np.concatenate([rc, rc_ctx], axis=0).astype(np.float32)
    rs = np.concatenate([rs, np.zeros((n_ctx, HEAD_PAD))], axis=0).astype(np.float32)
    return rc, rs, np.ascontiguousarray(rc.T), np.ascontiguousarray(rs.T)


def _swap_rope_columns(w):
    idx = jnp.arange(QK_ROPE) ^ (QK_ROPE // 4)
    return w[..., idx]


def _projection_weights(w_in, w_q_b, w_kv_b):
    depth, d, _ = w_in.shape
    w_in, w_q_b, w_kv_b = (w.astype(BF16) for w in (w_in, w_q_b, w_kv_b))
    q_end = Q_LORA
    kv_end = q_end + KV_LORA + QK_ROPE
    sc_end = kv_end + 2 * CONV_CH + 3 * SC_CH
    w_kr = w_in[..., q_end + KV_LORA:kv_end]
    assert HEAD_PAD == QK_NOPE + 2 * QK_ROPE
    pad_lo = jnp.zeros((depth, d, QK_NOPE), w_in.dtype)
    wa = jnp.concatenate([
        w_in[..., :q_end + KV_LORA], w_in[..., kv_end:sc_end],
        pad_lo, w_kr, _swap_rope_columns(w_kr)], axis=-1).astype(BF16)
    wg = _pad_cols(w_in[..., sc_end:])

    wq = w_q_b.reshape(depth, Q_LORA, MLA_HEADS, QK_NOPE + QK_ROPE)
    wqm = jnp.concatenate([wq, _swap_rope_columns(wq[..., QK_NOPE:])], axis=-1)
    wqm = wqm.reshape(depth, Q_LORA, MLA_HEADS * HEAD_PAD)

    wkv = w_kv_b.reshape(depth, KV_LORA, MLA_HEADS, QK_NOPE + V_HEAD)
    wk = jnp.concatenate(
        [wkv[..., :QK_NOPE],
         jnp.zeros((depth, KV_LORA, MLA_HEADS, HEAD_PAD - QK_NOPE), w_kv_b.dtype)], axis=-1)
    wk = wk.reshape(depth, KV_LORA, MLA_HEADS * HEAD_PAD).astype(BF16)
    wv = jnp.concatenate(
        [wkv[..., QK_NOPE:],
         jnp.zeros((depth, KV_LORA, MLA_HEADS, V_PAD - V_HEAD), w_kv_b.dtype)], axis=-1)
    wvt = jnp.swapaxes(wv.reshape(depth, KV_LORA, MLA_HEADS * V_PAD), 1, 2).astype(BF16)
    return wa, wg, jnp.swapaxes(wqm, 1, 2).astype(BF16), wk, wvt


def kernel(x, c, ctx, c_ctx, w_mod, b_mod, ln1_g, w_in, q_a_norm_g, w_q_b, kv_a_norm_g, w_kv_b, w_mla_o, conv_dw, conv_dw_b, conv_ln_g, conv_ln_b, w_conv_o, sc_dw, w_sc_o, w_o, ln2_g, w_ff1, w_ff3, w_ff2, final_g):
    b, n_lat, d = x.shape
    n_ctx = ctx.shape[1]
    depth = w_mod.shape[0]
    assert n_ctx == TM and n_lat % TM == 0 and n_lat % KC == 0 and b + 1 <= MOD_ROWS
    n_lat_tiles = n_lat // TM
    n_all_tiles = (n_lat + n_ctx) // TM

    act = jnp.concatenate([c, c_ctx[None, :], jnp.zeros((MOD_ROWS - b - 1, d), F32)], axis=0)
    mods = _modulation(act, w_mod, b_mod).reshape(depth, MOD_ROWS, N_MOD, d)

    rc, rs, rct, rst = _rope_patterns(n_lat, n_ctx)
    vec = lambda v: v.reshape(depth, 1, -1)
    wa, wg, wqmt, wk, wvt = _projection_weights(w_in, w_q_b, w_kv_b)
    wmo, wco, wso, wo = (_pad_cols(w) for w in (w_mla_o, w_conv_o, w_sc_o, w_o))
    w1, w3, w2 = (w.astype(BF16) for w in (w_ff1, w_ff3, w_ff2))

    xs = (x, ctx)
    for l in range(depth):
        last = l == depth - 1
        qt, k, vt, glu, cx, bg = _inproj(
            l, xs, mods, vec(ln1_g), wa, vec(q_a_norm_g), wqmt, vec(kv_a_norm_g), wk, wvt,
            rc, rs, rct, rst, n_all_tiles, n_lat_tiles)
        atts = (_attention(qt, k, vt, 0, n_lat_tiles, 0),)
        if not last:
            atts += (_attention(qt, k, vt, n_lat_tiles, n_all_tiles - n_lat_tiles, n_lat),)
        n_tiles = n_lat_tiles if last else n_all_tiles
        xa = _merge(l, xs, atts, mods, glu, cx, bg, vec(ln1_g), wg, wmo, conv_dw, vec(conv_dw_b),
                    vec(conv_ln_g), vec(conv_ln_b), wco, sc_dw, wso, wo, n_tiles, n_lat_tiles)
        xa = _ffn(l, xa, mods, vec(ln2_g), w1, w3, w2, final_g.reshape(1, -1), n_tiles,
                  n_lat_tiles, last)
        xs = (xa,)
    return xa
```

```python
import functools
import math

import jax
import jax.numpy as jnp
import numpy as np
from jax import lax
from jax.experimental import pallas as pl
from jax.experimental.pallas import tpu as pltpu

F32 = jnp.float32
BF16 = jnp.bfloat16

GRID_W = 64
MLA_HEADS = 8
QK_NOPE = 64
QK_ROPE = 32
V_HEAD = 64
Q_LORA = 384
KV_LORA = 256
CONV_CH = 256
CONV_WIDTH = 31
SC_CH = 256
SC_WIDTH = 3
N_MOD = 6
ROPE_THETA = 10000.0
EPS = 1e-6

LANES = 128
SUBLANES = 8
HEAD_PAD = LANES
TM = 256
HALO = 16
KC = 256
V_PAD = 80
M_INIT = -1e30
ATTN_HEADS = 2
ATTN_Q_TILES = 4
FFN_TILES = 3
MOD_ROWS = 8
MOD_TN = 1536
VMEM_LIMIT = 56 * 1024 * 1024

Q_SCALE = (QK_NOPE + QK_ROPE) ** -0.5 * math.log2(math.e)

A_Q = 0
A_KV = A_Q + Q_LORA
A_GLU_A = A_KV + KV_LORA
A_GLU_G = A_GLU_A + CONV_CH
A_SC_B = A_GLU_G + CONV_CH
A_SC_C = A_SC_B + SC_CH
A_SC_X = A_SC_C + SC_CH
A_KR = A_SC_X + SC_CH
A_COLS = A_KR + HEAD_PAD


def _dot(a, b):
    return jnp.dot(a, b, preferred_element_type=F32)


def _rms(x, g):
    return x * lax.rsqrt(jnp.mean(x * x, axis=-1, keepdims=True) + EPS) * g


def _norm_modulate(x, g, shift, scale):
    return _rms(x, g) * (1.0 + scale) + shift


def _sigmoid(x):
    return 0.5 * jnp.tanh(0.5 * x) + 0.5


def _pad_cols(w):
    return jnp.pad(w, ((0, 0),) * (w.ndim - 1) + ((0, LANES),)).astype(BF16)


def _mod_kernel(act_ref, w_ref, b_ref, o_ref):
    a = act_ref[...]
    a = a * _sigmoid(a)
    o_ref[0] = _dot(a.astype(BF16), w_ref[0].astype(BF16)) + b_ref[0]


def _modulation(act, w_mod, b_mod):
    depth, d, n = w_mod.shape
    return pl.pallas_call(
        _mod_kernel,
        out_shape=jax.ShapeDtypeStruct((depth, MOD_ROWS, n), F32),
        grid=(depth, n // MOD_TN),
        in_specs=[
            pl.BlockSpec((MOD_ROWS, d), lambda l, j: (0, 0)),
            pl.BlockSpec((1, d, MOD_TN), lambda l, j: (l, 0, j)),
            pl.BlockSpec((1, 1, MOD_TN), lambda l, j: (l, 0, j)),
        ],
        out_specs=pl.BlockSpec((1, MOD_ROWS, MOD_TN), lambda l, j: (l, 0, j)),
        compiler_params=pltpu.CompilerParams(
            dimension_semantics=("arbitrary", "arbitrary"), vmem_limit_bytes=VMEM_LIMIT),
        name="modulation",
    )(act, w_mod, b_mod.reshape(depth, 1, n))


def _dot_t(a, b):
    return lax.dot_general(a, b, (((1,), (1,)), ((), ())), preferred_element_type=F32)


def _tile_rows(lat_ref, ctx_ref, n_lat_tiles):
    if ctx_ref is None:
        return lat_ref[0]
    return jnp.where(pl.program_id(1) >= n_lat_tiles, ctx_ref[0], lat_ref[0])


def _inproj_kernel(*refs, n_lat_tiles, split):
    x_ref, ctx_ref = (refs[0], refs[1]) if split else (refs[0], None)
    (mod_ref, lng_ref, wa_ref, qg_ref, wqmt_ref, kvg_ref, wk_ref, wvt_ref, rc_ref, rs_ref,
     rct_ref, rst_ref, qt_ref, k_ref, vt_ref, glu_ref, cx_ref, bg_ref) = refs[2 if split else 1:]
    x = _tile_rows(x_ref, ctx_ref, n_lat_tiles)
    h = _norm_modulate(x, lng_ref[...], mod_ref[0:1, :], mod_ref[1:2, :])
    proj = _dot(h.astype(BF16), wa_ref[:, 0:A_COLS])
    rope = slice(QK_NOPE, QK_NOPE + QK_ROPE)
    swap = slice(QK_NOPE + QK_ROPE, HEAD_PAD)

    qn = _rms(proj[:, A_Q:A_Q + Q_LORA], qg_ref[...] * Q_SCALE).astype(BF16)
    qm = _dot_t(wqmt_ref[...], qn)
    cos_t, sin_t = rct_ref[rope, :], rst_ref[rope, :]
    pieces = []
    for hh in range(MLA_HEADS):
        head = qm[hh * HEAD_PAD:(hh + 1) * HEAD_PAD]
        pieces += [head[0:QK_NOPE], head[rope] * cos_t + head[swap] * sin_t,
                   jnp.zeros((HEAD_PAD - QK_NOPE - QK_ROPE, TM), F32)]
    qt_ref[0] = jnp.concatenate(pieces, axis=0).astype(BF16)

    cn = _rms(proj[:, A_KV:A_KV + KV_LORA], kvg_ref[...]).astype(BF16)
    kr = proj[:, A_KR:A_KR + HEAD_PAD]
    kr = kr * rc_ref[...] + pltpu.roll(kr, HEAD_PAD - QK_ROPE, axis=1) * rs_ref[...]
    k_ref[0] = (_dot(cn, wk_ref[...]) + jnp.concatenate([kr] * MLA_HEADS, axis=1)).astype(BF16)

    vt = _dot_t(wvt_ref[...], cn)
    rowid = lax.broadcasted_iota(jnp.int32, vt.shape, 0)
    vt_ref[0] = jnp.where(rowid % V_PAD == V_HEAD, 1.0, vt).astype(BF16)

    glu_ref[0] = proj[:, A_GLU_A:A_GLU_A + CONV_CH] * _sigmoid(proj[:, A_GLU_G:A_GLU_G + CONV_CH])
    bg_ref[0] = proj[:, A_SC_B:A_SC_B + SC_CH]
    cx_ref[0] = proj[:, A_SC_C:A_SC_C + SC_CH] * proj[:, A_SC_X:A_SC_X + SC_CH]


def _stream_specs(arrs, n_lat_tiles):
    width = arrs[0].shape[-1]
    if len(arrs) == 1:
        return [pl.BlockSpec((1, TM, width), lambda bi, i: (bi, i, 0))]
    assert arrs[1].shape[1] == TM
    return [pl.BlockSpec((1, TM, width), lambda bi, i: (bi, jnp.minimum(i, n_lat_tiles - 1), 0)),
            pl.BlockSpec((1, TM, width), lambda bi, i: (bi, 0, 0))]


def _layer_spec(l, arr):
    zeros = (0,) * (arr.ndim - 1)
    return pl.BlockSpec((None,) + arr.shape[1:], lambda bi, i: (l,) + zeros)


def _col_spec(l, arr, width, col_block):
    return pl.BlockSpec((None, arr.shape[1], width), lambda bi, i: (l, 0, col_block))


def _mod_spec(l, mods, n_lat_tiles, ctx_row):
    _, _, n_mod, d = mods.shape
    return pl.BlockSpec((None, None, n_mod, d),
                        lambda bi, i: (l, jnp.where(i < n_lat_tiles, bi, ctx_row), 0, 0))


def _inproj(l, xs, mods, lng, w_all, qg, wqmt, kvg, wk, wvt, rc, rs, rct, rst, n_tiles,
            n_lat_tiles):
    b, _, d = xs[0].shape
    t_all = n_tiles * TM
    hq = MLA_HEADS * HEAD_PAD
    hv = MLA_HEADS * V_PAD
    row = lambda bi, i: (bi, i, 0)
    col = lambda bi, i: (bi, 0, i)
    params = (qg, wqmt, kvg, wk, wvt)
    return pl.pallas_call(
        functools.partial(_inproj_kernel, n_lat_tiles=n_lat_tiles, split=len(xs) == 2),
        out_shape=(
            jax.ShapeDtypeStruct((b, hq, t_all), BF16),
            jax.ShapeDtypeStruct((b, t_all, hq), BF16),
            jax.ShapeDtypeStruct((b, hv, t_all), BF16),
            jax.ShapeDtypeStruct((b, t_all, CONV_CH), F32),
            jax.ShapeDtypeStruct((b, t_all, SC_CH), F32),
            jax.ShapeDtypeStruct((b, t_all, SC_CH), F32),
        ),
        grid=(b, n_tiles),
        in_specs=(
            _stream_specs(xs, n_lat_tiles)
            + [_mod_spec(l, mods, n_lat_tiles, b), _layer_spec(l, lng),
               _col_spec(l, w_all, w_all.shape[-1] // 2, 1)]
            + [_layer_spec(l, p) for p in params]
            + [pl.BlockSpec((TM, HEAD_PAD), lambda bi, i: (i, 0)),
               pl.BlockSpec((TM, HEAD_PAD), lambda bi, i: (i, 0)),
               pl.BlockSpec((HEAD_PAD, TM), lambda bi, i: (0, i)),
               pl.BlockSpec((HEAD_PAD, TM), lambda bi, i: (0, i))]),
        out_specs=(
            pl.BlockSpec((1, hq, TM), col),
            pl.BlockSpec((1, TM, hq), row),
            pl.BlockSpec((1, hv, TM), col),
            pl.BlockSpec((1, TM, CONV_CH), row),
            pl.BlockSpec((1, TM, SC_CH), row),
            pl.BlockSpec((1, TM, SC_CH), row),
        ),
        compiler_params=pltpu.CompilerParams(
            dimension_semantics=("arbitrary", "arbitrary"), vmem_limit_bytes=VMEM_LIMIT),
        name="inproj",
    )(*xs, mods, lng, w_all, *params, rc, rs, rct, rst)


def _attn_kernel(qt_ref, k_ref, vt_ref, o_ref):
    for sub in range(qt_ref.shape[2] // TM):
        _attn_tile(qt_ref, k_ref, vt_ref, o_ref, sub * TM)


def _attn_tile(qt_ref, k_ref, vt_ref, o_ref, q0):
    n_keys = k_ref.shape[1]
    chunks = [(s, min(KC, n_keys - s)) for s in range(0, n_keys, KC)]
    heads = range(ATTN_HEADS)
    qts = [qt_ref[0, hh * HEAD_PAD:(hh + 1) * HEAD_PAD, q0:q0 + TM] for hh in heads]

    def scores(chunk):
        s0, n = chunk
        return tuple(_dot(k_ref[0, s0:s0 + n, hh * HEAD_PAD:(hh + 1) * HEAD_PAD], qts[hh])
                     for hh in heads)

    def softmax(sts, ms):
        out = []
        for hh in heads:
            m_new = jnp.maximum(ms[hh], jnp.max(sts[hh], axis=0, keepdims=True))
            out.append((m_new, jnp.exp2(ms[hh] - m_new), jnp.exp2(sts[hh] - m_new).astype(BF16)))
        return tuple(o[0] for o in out), tuple(o[1:] for o in out)

    def accumulate(chunk, pts, accs):
        s0, n = chunk
        return tuple(
            pts[hh][0] * accs[hh] + _dot(vt_ref[0, hh * V_PAD:(hh + 1) * V_PAD, s0:s0 + n],
                                         pts[hh][1])
            for hh in heads)

    ms = tuple(jnp.full((1, TM), M_INIT, F32) for _ in heads)
    accs = tuple(jnp.zeros((V_PAD, TM), F32) for _ in heads)
    sts, pts = scores(chunks[0]), None
    for t in range(1, len(chunks) + 2):
        new_pts = None
        if t - 1 < len(chunks):
            ms, new_pts = softmax(sts, ms)
        sts = scores(chunks[t]) if t < len(chunks) else None
        if pts is not None:
            accs = accumulate(chunks[t - 2], pts, accs)
        pts = new_pts
    outs = [(acc[0:V_HEAD] / acc[V_HEAD:V_HEAD + 1]).T for acc in accs]
    o_ref[0, q0:q0 + TM, :] = jnp.concatenate(outs, axis=1).astype(BF16)


def _attention(qt, k, vt, q_tile_lo, n_q_tiles, first_key):
    b, t_all, _ = k.shape
    groups = MLA_HEADS // ATTN_HEADS
    per = ATTN_Q_TILES if n_q_tiles % ATTN_Q_TILES == 0 and q_tile_lo % ATTN_Q_TILES == 0 else 1
    tq = per * TM
    lo = q_tile_lo // per
    n_keys = t_all - first_key
    assert first_key % n_keys == 0
    key_blk = first_key // n_keys
    return pl.pallas_call(
        _attn_kernel,
        out_shape=jax.ShapeDtypeStruct((b, n_q_tiles * TM, MLA_HEADS * V_HEAD), BF16),
        grid=(b, groups, n_q_tiles // per),
        in_specs=[
            pl.BlockSpec((1, ATTN_HEADS * HEAD_PAD, tq), lambda bi, j, i: (bi, j, i + lo)),
            pl.BlockSpec((1, n_keys, ATTN_HEADS * HEAD_PAD), lambda bi, j, i: (bi, key_blk, j)),
            pl.BlockSpec((1, ATTN_HEADS * V_PAD, n_keys), lambda bi, j, i: (bi, j, key_blk)),
        ],
        out_specs=pl.BlockSpec((1, tq, ATTN_HEADS * V_HEAD), lambda bi, j, i: (bi, i, j)),
        compiler_params=pltpu.CompilerParams(
            dimension_semantics=("arbitrary", "arbitrary", "arbitrary"), vmem_limit_bytes=VMEM_LIMIT),
        name="attention",
    )(qt, k, vt)


def _fill_halo_buffer(buf, cur_ref, prev_ref, next_ref, keep_prev, keep_next):
    buf[0:HALO, :] = prev_ref[0] * keep_prev
    buf[HALO:HALO + TM, :] = cur_ref[0]
    buf[HALO + TM:HALO + TM + HALO, :] = next_ref[0] * keep_next


def _depthwise(buf, w_ref, width):
    base = HALO - (width - 1) // 2
    out = None
    for r in range(SUBLANES):
        acc = None
        for k in range(width):
            if (base + k) % SUBLANES != r:
                continue
            off = base + k - r
            term = w_ref[k:k + 1, :] * buf[off:off + TM + SUBLANES, :]
            acc = term if acc is None else acc + term
        if acc is not None:
            out = acc[r:r + TM, :] if out is None else out + acc[r:r + TM, :]
    return out


def _merge_kernel(*refs, n_lat_tiles, x_split, att_split):
    refs = list(refs)
    x_ref = refs.pop(0)
    xc_ref = refs.pop(0) if x_split else None
    att_ref = refs.pop(0)
    attc_ref = refs.pop(0) if att_split else None
    (mod_ref, glu_ref, glu_p_ref, glu_n_ref, cx_ref, cx_p_ref, cx_n_ref, bg_ref, lng_ref, wg_ref,
     wmo_ref, dw_ref, dwb_ref, cg_ref, cb_ref, wco_ref, scw_ref, wso_ref, wo_ref, o_ref, cbuf,
     sbuf) = refs
    i = pl.program_id(1)
    is_ctx = i >= n_lat_tiles
    keep_prev = jnp.where((i == 0) | is_ctx, 0.0, 1.0)
    keep_next = jnp.where((i == n_lat_tiles - 1) | is_ctx, 0.0, 1.0)
    _fill_halo_buffer(cbuf, glu_ref, glu_p_ref, glu_n_ref, keep_prev, keep_next)
    _fill_halo_buffer(sbuf, cx_ref, cx_p_ref, cx_n_ref, keep_prev, keep_next)

    x = _tile_rows(x_ref, xc_ref, n_lat_tiles)
    d = x.shape[-1]
    h = _norm_modulate(x, lng_ref[...], mod_ref[0:1, :], mod_ref[1:2, :]).astype(BF16)

    def gate(branch):
        return _sigmoid(_dot(h, wg_ref[:, branch * d:(branch + 1) * d]))

    merged = gate(0) * _dot(_tile_rows(att_ref, attc_ref, n_lat_tiles), wmo_ref[:, 0:d])

    u = _depthwise(cbuf, dw_ref, CONV_WIDTH) + dwb_ref[...]
    mu = jnp.mean(u, axis=-1, keepdims=True)
    uc = u - mu
    var = jnp.mean(uc * uc, axis=-1, keepdims=True)
    u = uc * lax.rsqrt(var + EPS) * cg_ref[...] + cb_ref[...]
    u = u * _sigmoid(u)
    merged = merged + gate(1) * _dot(u.astype(BF16), wco_ref[:, 0:d])

    sc = bg_ref[0] * _depthwise(sbuf, scw_ref, SC_WIDTH)
    merged = merged + gate(2) * _dot(sc.astype(BF16), wso_ref[:, 0:d])

    o_ref[0] = x + mod_ref[2:3, :] * _dot(merged.astype(BF16), wo_ref[:, 0:d])


def _merge(l, xs, atts, mods, glu, cx, bg, lng, w_all, wmo, dw, dwb, cg, cb, wco, scw, wso, wo,
           n_tiles, n_lat_tiles):
    b, _, d = xs[0].shape
    per = TM // HALO
    last_halo = glu.shape[1] // HALO - 1
    row = lambda bi, i: (bi, i, 0)
    prev = lambda bi, i: (bi, jnp.maximum(i * per - 1, 0), 0)
    nxt = lambda bi, i: (bi, jnp.minimum((i + 1) * per, last_halo), 0)
    params = (wmo, dw, dwb, cg, cb, wco, scw, wso, wo)
    gate_spec = _col_spec(l, w_all, w_all.shape[-1] // 2, 0)
    return pl.pallas_call(
        functools.partial(_merge_kernel, n_lat_tiles=n_lat_tiles, x_split=len(xs) == 2,
                          att_split=len(atts) == 2),
        out_shape=jax.ShapeDtypeStruct((b, n_tiles * TM, d), F32),
        grid=(b, n_tiles),
        in_specs=(
            _stream_specs(xs, n_lat_tiles)
            + _stream_specs(atts, n_lat_tiles)
            + [_mod_spec(l, mods, n_lat_tiles, b),
               pl.BlockSpec((1, TM, CONV_CH), row),
               pl.BlockSpec((1, HALO, CONV_CH), prev),
               pl.BlockSpec((1, HALO, CONV_CH), nxt),
               pl.BlockSpec((1, TM, SC_CH), row),
               pl.BlockSpec((1, HALO, SC_CH), prev),
               pl.BlockSpec((1, HALO, SC_CH), nxt),
               pl.BlockSpec((1, TM, SC_CH), row),
               _layer_spec(l, lng), gate_spec]
            + [_layer_spec(l, p) for p in params]),
        out_specs=pl.BlockSpec((1, TM, d), row),
        scratch_shapes=[
            pltpu.VMEM((TM + 2 * HALO, CONV_CH), F32),
            pltpu.VMEM((TM + 2 * HALO, SC_CH), F32),
        ],
        compiler_params=pltpu.CompilerParams(
            dimension_semantics=("arbitrary", "arbitrary"), vmem_limit_bytes=VMEM_LIMIT),
        name="merge",
    )(*xs, *atts, mods, glu, glu, glu, cx, cx, cx, bg, lng, w_all, *params)


def _ffn_kernel(x_ref, mod_ref, modc_ref, lng_ref, w1_ref, w3_ref, w2_ref, fg_ref, o_ref, *,
                n_lat_tiles, final_norm):
    per = x_ref.shape[1] // TM
    for sub in range(per):
        rows = slice(sub * TM, (sub + 1) * TM)
        is_ctx = pl.program_id(1) * per + sub >= n_lat_tiles
        mod = jnp.where(is_ctx, modc_ref[...], mod_ref[...])
        x = x_ref[0, rows, :]
        h = _norm_modulate(x, lng_ref[...], mod[3:4, :], mod[4:5, :]).astype(BF16)
        a = _dot(h, w1_ref[...])
        a = a * _sigmoid(a) * _dot(h, w3_ref[...])
        y = x + mod[5:6, :] * _dot(a.astype(BF16), w2_ref[...])
        if final_norm:
            y = _rms(y, fg_ref[...])
        o_ref[0, rows, :] = y


def _ffn(l, xa, mods, lng, w1, w3, w2, fg, n_tiles, n_lat_tiles, final_norm):
    b, _, d = xa.shape
    per = max(p for p in range(1, FFN_TILES + 1) if n_tiles % p == 0)
    row = lambda bi, i: (bi, i, 0)
    n_mod = mods.shape[2]
    return pl.pallas_call(
        functools.partial(_ffn_kernel, n_lat_tiles=n_lat_tiles, final_norm=final_norm),
        out_shape=jax.ShapeDtypeStruct((b, n_tiles * TM, d), F32),
        grid=(b, n_tiles // per),
        in_specs=[
            pl.BlockSpec((1, per * TM, d), row),
            pl.BlockSpec((None, None, n_mod, d), lambda bi, i: (l, bi, 0, 0)),
            pl.BlockSpec((None, None, n_mod, d), lambda bi, i: (l, b, 0, 0)),
            _layer_spec(l, lng),
            _layer_spec(l, w1),
            _layer_spec(l, w3),
            _layer_spec(l, w2),
            pl.BlockSpec((1, d), lambda bi, i: (0, 0)),
        ],
        out_specs=pl.BlockSpec((1, per * TM, d), row),
        compiler_params=pltpu.CompilerParams(
            dimension_semantics=("arbitrary", "arbitrary"), vmem_limit_bytes=VMEM_LIMIT),
        name="ffn",
    )(xa, mods, mods, lng, w1, w3, w2, fg)


def _rope_patterns(n_lat, n_ctx):
    t = np.arange(n_lat)
    half = QK_ROPE // 2
    inv = np.float64(ROPE_THETA) ** (-np.arange(0, half, 2, dtype=np.float64) / half)
    ang_r = (t // GRID_W)[:, None] * inv
    ang_c = (t % GRID_W)[:, None] * inv
    cos = np.concatenate([np.cos(ang_r)] * 2 + [np.cos(ang_c)] * 2, axis=1)
    sin = np.concatenate([-np.sin(ang_r), np.sin(ang_r), -np.sin(ang_c), np.sin(ang_c)], axis=1)
    pad = HEAD_PAD - QK_NOPE - QK_ROPE
    rc = np.concatenate([np.ones((n_lat, QK_NOPE)), cos, np.zeros((n_lat, pad))], axis=1)
    rs = np.concatenate([np.zeros((n_lat, QK_NOPE)), sin, np.zeros((n_lat, pad))], axis=1)
    rc_ctx = np.concatenate([np.ones((n_ctx, QK_NOPE + QK_ROPE)), np.zeros((n_ctx, pad))], axis=1)
    rc = np.concatenate([rc, rc_ctx], axis=0).astype(np.float32)
    rs = np.concatenate([rs, np.zeros((n_ctx, HEAD_PAD))], axis=0).astype(np.float32)
    return rc, rs, np.ascontiguousarray(rc.T), np.ascontiguousarray(rs.T)


def _swap_rope_columns(w):
    idx = jnp.arange(QK_ROPE) ^ (QK_ROPE // 4)
    return w[..., idx]


def _projection_weights(w_in, w_q_b, w_kv_b):
    depth, d, _ = w_in.shape
    w_in, w_q_b, w_kv_b = (w.astype(BF16) for w in (w_in, w_q_b, w_kv_b))
    q_end = Q_LORA
    kv_end = q_end + KV_LORA + QK_ROPE
    sc_end = kv_end + 2 * CONV_CH + 3 * SC_CH
    w_kr = w_in[..., q_end + KV_LORA:kv_end]
    assert HEAD_PAD == QK_NOPE + 2 * QK_ROPE
    n_gate = w_in.shape[-1] - sc_end
    w_block = n_gate + LANES
    assert A_COLS <= w_block
    w_all = jnp.concatenate([
        w_in[..., sc_end:], jnp.zeros((depth, d, w_block - n_gate), w_in.dtype),
        w_in[..., :q_end + KV_LORA], w_in[..., kv_end:sc_end],
        jnp.zeros((depth, d, QK_NOPE), w_in.dtype), w_kr, _swap_rope_columns(w_kr),
        jnp.zeros((depth, d, w_block - A_COLS), w_in.dtype)], axis=-1)

    wq = w_q_b.reshape(depth, Q_LORA, MLA_HEADS, QK_NOPE + QK_ROPE)
    wqm = jnp.concatenate([wq, _swap_rope_columns(wq[..., QK_NOPE:])], axis=-1)
    wqm = wqm.reshape(depth, Q_LORA, MLA_HEADS * HEAD_PAD)

    wkv = w_kv_b.reshape(depth, KV_LORA, MLA_HEADS, QK_NOPE + V_HEAD)
    wk = jnp.concatenate(
        [wkv[..., :QK_NOPE],
         jnp.zeros((depth, KV_LORA, MLA_HEADS, HEAD_PAD - QK_NOPE), w_kv_b.dtype)], axis=-1)
    wk = wk.reshape(depth, KV_LORA, MLA_HEADS * HEAD_PAD).astype(BF16)
    wv = jnp.concatenate(
        [wkv[..., QK_NOPE:],
         jnp.zeros((depth, KV_LORA, MLA_HEADS, V_PAD - V_HEAD), w_kv_b.dtype)], axis=-1)
    wvt = jnp.swapaxes(wv.reshape(depth, KV_LORA, MLA_HEADS * V_PAD), 1, 2).astype(BF16)
    return w_all, jnp.swapaxes(wqm, 1, 2).astype(BF16), wk, wvt


def kernel(x, c, ctx, c_ctx, w_mod, b_mod, ln1_g, w_in, q_a_norm_g, w_q_b, kv_a_norm_g, w_kv_b, w_mla_o, conv_dw, conv_dw_b, conv_ln_g, conv_ln_b, w_conv_o, sc_dw, w_sc_o, w_o, ln2_g, w_ff1, w_ff3, w_ff2, final_g):
    b, n_lat, d = x.shape
    n_ctx = ctx.shape[1]
    depth = w_mod.shape[0]
    assert n_ctx == TM and n_lat % TM == 0 and n_lat % KC == 0 and b + 1 <= MOD_ROWS
    n_lat_tiles = n_lat // TM
    n_all_tiles = (n_lat + n_ctx) // TM

    act = jnp.concatenate([c, c_ctx[None, :], jnp.zeros((MOD_ROWS - b - 1, d), F32)], axis=0)
    mods = _modulation(act, w_mod, b_mod).reshape(depth, MOD_ROWS, N_MOD, d)

    rc, rs, rct, rst = _rope_patterns(n_lat, n_ctx)
    vec = lambda v: v.reshape(depth, 1, -1)
    w_all, wqmt, wk, wvt = _projection_weights(w_in, w_q_b, w_kv_b)
    wmo, wco, wso, wo = (_pad_cols(w) for w in (w_mla_o, w_conv_o, w_sc_o, w_o))
    w1, w3, w2 = (w.astype(BF16) for w in (w_ff1, w_ff3, w_ff2))

    xs = (x, ctx)
    for l in range(depth):
        last = l == depth - 1
        qt, k, vt, glu, cx, bg = _inproj(
            l, xs, mods, vec(ln1_g), w_all, vec(q_a_norm_g), wqmt, vec(kv_a_norm_g), wk, wvt,
            rc, rs, rct, rst, n_all_tiles, n_lat_tiles)
        atts = (_attention(qt, k, vt, 0, n_lat_tiles, 0),)
        if not last:
            atts += (_attention(qt, k, vt, n_lat_tiles, n_all_tiles - n_lat_tiles, n_lat),)
        n_tiles = n_lat_tiles if last else n_all_tiles
        xa = _merge(l, xs, atts, mods, glu, cx, bg, vec(ln1_g), w_all, wmo, conv_dw, vec(conv_dw_b),
                    vec(conv_ln_g), vec(conv_ln_b), wco, sc_dw, wso, wo, n_tiles, n_lat_tiles)
        xa = _ffn(l, xa, mods, vec(ln2_g), w1, w3, w2, final_g.reshape(1, -1), n_tiles,
                  n_lat_tiles, last)
        xs = (xa,)
    return xa
```

```python
import functools
import math

import jax
import jax.numpy as jnp
import numpy as np
from jax import lax
from jax.experimental import pallas as pl
from jax.experimental.pallas import tpu as pltpu

F32 = jnp.float32
BF16 = jnp.bfloat16

GRID_W = 64
MLA_HEADS = 8
QK_NOPE = 64
QK_ROPE = 32
V_HEAD = 64
Q_LORA = 384
KV_LORA = 256
CONV_CH = 256
CONV_WIDTH = 31
SC_CH = 256
SC_WIDTH = 3
N_MOD = 6
ROPE_THETA = 10000.0
EPS = 1e-6

LANES = 128
SUBLANES = 8
HEAD_PAD = LANES
TM = 256
HALO = 16
KC = 256
V_PAD = 80
M_INIT = -1e30
ATTN_HEADS = 2
ATTN_Q_TILES = 4
FFN_TILES = 4
MOD_ROWS = 8
MOD_TN = 1536
VMEM_LIMIT = 56 * 1024 * 1024

Q_SCALE = (QK_NOPE + QK_ROPE) ** -0.5 * math.log2(math.e)

A_Q = 0
A_KV = A_Q + Q_LORA
A_GLU_A = A_KV + KV_LORA
A_GLU_G = A_GLU_A + CONV_CH
A_SC_B = A_GLU_G + CONV_CH
A_SC_C = A_SC_B + SC_CH
A_SC_X = A_SC_C + SC_CH
A_KR = A_SC_X + SC_CH
A_COLS = A_KR + HEAD_PAD


def _dot(a, b):
    return jnp.dot(a, b, preferred_element_type=F32)


def _rms(x, g):
    return x * lax.rsqrt(jnp.mean(x * x, axis=-1, keepdims=True) + EPS) * g


def _norm_modulate(x, g, shift, scale):
    return _rms(x, g) * (1.0 + scale) + shift


def _sigmoid(x):
    return 0.5 * jnp.tanh(0.5 * x) + 0.5


def _pad_cols(w):
    return jnp.pad(w, ((0, 0),) * (w.ndim - 1) + ((0, LANES),)).astype(BF16)


def _mod_kernel(act_ref, w_ref, b_ref, o_ref):
    a = act_ref[...]
    a = a * _sigmoid(a)
    o_ref[0] = _dot(a.astype(BF16), w_ref[0].astype(BF16)) + b_ref[0]


def _modulation(act, w_mod, b_mod):
    depth, d, n = w_mod.shape
    return pl.pallas_call(
        _mod_kernel,
        out_shape=jax.ShapeDtypeStruct((depth, MOD_ROWS, n), F32),
        grid=(depth, n // MOD_TN),
        in_specs=[
            pl.BlockSpec((MOD_ROWS, d), lambda l, j: (0, 0)),
            pl.BlockSpec((1, d, MOD_TN), lambda l, j: (l, 0, j)),
            pl.BlockSpec((1, 1, MOD_TN), lambda l, j: (l, 0, j)),
        ],
        out_specs=pl.BlockSpec((1, MOD_ROWS, MOD_TN), lambda l, j: (l, 0, j)),
        compiler_params=pltpu.CompilerParams(
            dimension_semantics=("arbitrary", "arbitrary"), vmem_limit_bytes=VMEM_LIMIT),
        name="modulation",
    )(act, w_mod, b_mod.reshape(depth, 1, n))


def _dot_t(a, b):
    return lax.dot_general(a, b, (((1,), (1,)), ((), ())), preferred_element_type=F32)


def _tile_rows(lat_ref, ctx_ref, n_lat_tiles):
    if ctx_ref is None:
        return lat_ref[0]
    return jnp.where(pl.program_id(1) >= n_lat_tiles, ctx_ref[0], lat_ref[0])


def _inproj_kernel(*refs, n_lat_tiles, split):
    x_ref, ctx_ref = (refs[0], refs[1]) if split else (refs[0], None)
    (mod_ref, lng_ref, wa_ref, qg_ref, wqmt_ref, kvg_ref, wk_ref, wvt_ref, rc_ref, rs_ref,
     rct_ref, rst_ref, qt_ref, k_ref, vt_ref, glu_ref, cx_ref, bg_ref) = refs[2 if split else 1:]
    x = _tile_rows(x_ref, ctx_ref, n_lat_tiles)
    h = _norm_modulate(x, lng_ref[...], mod_ref[0:1, :], mod_ref[1:2, :])
    proj = _dot(h.astype(BF16), wa_ref[...])
    rope = slice(QK_NOPE, QK_NOPE + QK_ROPE)
    swap = slice(QK_NOPE + QK_ROPE, HEAD_PAD)

    qn = _rms(proj[:, A_Q:A_Q + Q_LORA], qg_ref[...] * Q_SCALE).astype(BF16)
    qm = _dot_t(wqmt_ref[...], qn)
    cos_t, sin_t = rct_ref[rope, :], rst_ref[rope, :]
    pieces = []
    for hh in range(MLA_HEADS):
        head = qm[hh * HEAD_PAD:(hh + 1) * HEAD_PAD]
        pieces += [head[0:QK_NOPE], head[rope] * cos_t + head[swap] * sin_t,
                   jnp.zeros((HEAD_PAD - QK_NOPE - QK_ROPE, TM), F32)]
    qt_ref[0] = jnp.concatenate(pieces, axis=0).astype(BF16)

    cn = _rms(proj[:, A_KV:A_KV + KV_LORA], kvg_ref[...]).astype(BF16)
    kr = proj[:, A_KR:A_KR + HEAD_PAD]
    kr = kr * rc_ref[...] + pltpu.roll(kr, HEAD_PAD - QK_ROPE, axis=1) * rs_ref[...]
    k_ref[0] = (_dot(cn, wk_ref[...]) + jnp.concatenate([kr] * MLA_HEADS, axis=1)).astype(BF16)

    vt = _dot_t(wvt_ref[...], cn)
    rowid = lax.broadcasted_iota(jnp.int32, vt.shape, 0)
    vt_ref[0] = jnp.where(rowid % V_PAD == V_HEAD, 1.0, vt).astype(BF16)

    glu_ref[0] = proj[:, A_GLU_A:A_GLU_A + CONV_CH] * _sigmoid(proj[:, A_GLU_G:A_GLU_G + CONV_CH])
    bg_ref[0] = proj[:, A_SC_B:A_SC_B + SC_CH]
    cx_ref[0] = proj[:, A_SC_C:A_SC_C + SC_CH] * proj[:, A_SC_X:A_SC_X + SC_CH]


def _stream_specs(arrs, n_lat_tiles):
    width = arrs[0].shape[-1]
    if len(arrs) == 1:
        return [pl.BlockSpec((1, TM, width), lambda bi, i: (bi, i, 0))]
    assert arrs[1].shape[1] == TM
    return [pl.BlockSpec((1, TM, width), lambda bi, i: (bi, jnp.minimum(i, n_lat_tiles - 1), 0)),
            pl.BlockSpec((1, TM, width), lambda bi, i: (bi, 0, 0))]


def _layer_spec(l, arr):
    zeros = (0,) * (arr.ndim - 1)
    return pl.BlockSpec((None,) + arr.shape[1:], lambda bi, i: (l,) + zeros)


def _mod_spec(l, mods, n_lat_tiles, ctx_row):
    _, _, n_mod, d = mods.shape
    return pl.BlockSpec((None, None, n_mod, d),
                        lambda bi, i: (l, jnp.where(i < n_lat_tiles, bi, ctx_row), 0, 0))


def _inproj(l, xs, mods, lng, wa, qg, wqmt, kvg, wk, wvt, rc, rs, rct, rst, n_tiles,
            n_lat_tiles):
    b, _, d = xs[0].shape
    t_all = n_tiles * TM
    hq = MLA_HEADS * HEAD_PAD
    hv = MLA_HEADS * V_PAD
    row = lambda bi, i: (bi, i, 0)
    col = lambda bi, i: (bi, 0, i)
    params = (lng, wa, qg, wqmt, kvg, wk, wvt)
    return pl.pallas_call(
        functools.partial(_inproj_kernel, n_lat_tiles=n_lat_tiles, split=len(xs) == 2),
        out_shape=(
            jax.ShapeDtypeStruct((b, hq, t_all), BF16),
            jax.ShapeDtypeStruct((b, t_all, hq), BF16),
            jax.ShapeDtypeStruct((b, hv, t_all), BF16),
            jax.ShapeDtypeStruct((b, t_all, CONV_CH), F32),
            jax.ShapeDtypeStruct((b, t_all, SC_CH), F32),
            jax.ShapeDtypeStruct((b, t_all, SC_CH), F32),
        ),
        grid=(b, n_tiles),
        in_specs=(
            _stream_specs(xs, n_lat_tiles)
            + [_mod_spec(l, mods, n_lat_tiles, b)]
            + [_layer_spec(l, p) for p in params]
            + [pl.BlockSpec((TM, HEAD_PAD), lambda bi, i: (i, 0)),
               pl.BlockSpec((TM, HEAD_PAD), lambda bi, i: (i, 0)),
               pl.BlockSpec((HEAD_PAD, TM), lambda bi, i: (0, i)),
               pl.BlockSpec((HEAD_PAD, TM), lambda bi, i: (0, i))]),
        out_specs=(
            pl.BlockSpec((1, hq, TM), col),
            pl.BlockSpec((1, TM, hq), row),
            pl.BlockSpec((1, hv, TM), col),
            pl.BlockSpec((1, TM, CONV_CH), row),
            pl.BlockSpec((1, TM, SC_CH), row),
            pl.BlockSpec((1, TM, SC_CH), row),
        ),
        compiler_params=pltpu.CompilerParams(
            dimension_semantics=("arbitrary", "arbitrary"), vmem_limit_bytes=VMEM_LIMIT),
        name="inproj",
    )(*xs, mods, *params, rc, rs, rct, rst)


def _attn_kernel(qt_ref, k_ref, vt_ref, o_ref):
    for sub in range(qt_ref.shape[2] // TM):
        _attn_tile(qt_ref, k_ref, vt_ref, o_ref, sub * TM)


def _attn_tile(qt_ref, k_ref, vt_ref, o_ref, q0):
    n_keys = k_ref.shape[1]
    chunks = [(s, min(KC, n_keys - s)) for s in range(0, n_keys, KC)]
    heads = range(ATTN_HEADS)
    qts = [qt_ref[0, hh * HEAD_PAD:(hh + 1) * HEAD_PAD, q0:q0 + TM] for hh in heads]

    def scores(chunk):
        s0, n = chunk
        return tuple(_dot(k_ref[0, s0:s0 + n, hh * HEAD_PAD:(hh + 1) * HEAD_PAD], qts[hh])
                     for hh in heads)

    def softmax(sts, ms):
        out = []
        for hh in heads:
            m_new = jnp.maximum(ms[hh], jnp.max(sts[hh], axis=0, keepdims=True))
            out.append((m_new, jnp.exp2(ms[hh] - m_new), jnp.exp2(sts[hh] - m_new).astype(BF16)))
        return tuple(o[0] for o in out), tuple(o[1:] for o in out)

    def accumulate(chunk, pts, accs):
        s0, n = chunk
        return tuple(
            pts[hh][0] * accs[hh] + _dot(vt_ref[0, hh * V_PAD:(hh + 1) * V_PAD, s0:s0 + n],
                                         pts[hh][1])
            for hh in heads)

    ms = tuple(jnp.full((1, TM), M_INIT, F32) for _ in heads)
    accs = tuple(jnp.zeros((V_PAD, TM), F32) for _ in heads)
    sts, pts = scores(chunks[0]), None
    for t in range(1, len(chunks) + 2):
        new_pts = None
        if t - 1 < len(chunks):
            ms, new_pts = softmax(sts, ms)
        sts = scores(chunks[t]) if t < len(chunks) else None
        if pts is not None:
            accs = accumulate(chunks[t - 2], pts, accs)
        pts = new_pts
    outs = [(acc[0:V_HEAD] / acc[V_HEAD:V_HEAD + 1]).T for acc in accs]
    o_ref[0, q0:q0 + TM, :] = jnp.concatenate(outs, axis=1).astype(BF16)


def _attention(qt, k, vt, q_tile_lo, n_q_tiles, first_key):
    b, t_all, _ = k.shape
    groups = MLA_HEADS // ATTN_HEADS
    per = ATTN_Q_TILES if n_q_tiles % ATTN_Q_TILES == 0 and q_tile_lo % ATTN_Q_TILES == 0 else 1
    tq = per * TM
    lo = q_tile_lo // per
    n_keys = t_all - first_key
    assert first_key % n_keys == 0
    key_blk = first_key // n_keys
    return pl.pallas_call(
        _attn_kernel,
        out_shape=jax.ShapeDtypeStruct((b, n_q_tiles * TM, MLA_HEADS * V_HEAD), BF16),
        grid=(b, groups, n_q_tiles // per),
        in_specs=[
            pl.BlockSpec((1, ATTN_HEADS * HEAD_PAD, tq), lambda bi, j, i: (bi, j, i + lo)),
            pl.BlockSpec((1, n_keys, ATTN_HEADS * HEAD_PAD), lambda bi, j, i: (bi, key_blk, j)),
            pl.BlockSpec((1, ATTN_HEADS * V_PAD, n_keys), lambda bi, j, i: (bi, j, key_blk)),
        ],
        out_specs=pl.BlockSpec((1, tq, ATTN_HEADS * V_HEAD), lambda bi, j, i: (bi, i, j)),
        compiler_params=pltpu.CompilerParams(
            dimension_semantics=("arbitrary", "arbitrary", "arbitrary"), vmem_limit_bytes=VMEM_LIMIT),
        name="attention",
    )(qt, k, vt)


def _fill_halo_buffer(buf, cur_ref, prev_ref, next_ref, keep_prev, keep_next):
    buf[0:HALO, :] = prev_ref[0] * keep_prev
    buf[HALO:HALO + TM, :] = cur_ref[0]
    buf[HALO + TM:HALO + TM + HALO, :] = next_ref[0] * keep_next


def _depthwise(buf, w_ref, width):
    base = HALO - (width - 1) // 2
    out = None
    for r in range(SUBLANES):
        acc = None
        for k in range(width):
            if (base + k) % SUBLANES != r:
                continue
            off = base + k - r
            term = w_ref[k:k + 1, :] * buf[off:off + TM + SUBLANES, :]
            acc = term if acc is None else acc + term
        if acc is not None:
            out = acc[r:r + TM, :] if out is None else out + acc[r:r + TM, :]
    return out


def _merge_kernel(*refs, n_lat_tiles, x_split, att_split):
    refs = list(refs)
    x_ref = refs.pop(0)
    xc_ref = refs.pop(0) if x_split else None
    att_ref = refs.pop(0)
    attc_ref = refs.pop(0) if att_split else None
    (mod_ref, glu_ref, glu_p_ref, glu_n_ref, cx_ref, cx_p_ref, cx_n_ref, bg_ref, lng_ref, wg_ref,
     wmo_ref, dw_ref, dwb_ref, cg_ref, cb_ref, wco_ref, scw_ref, wso_ref, wo_ref, o_ref, cbuf,
     sbuf) = refs
    i = pl.program_id(1)
    is_ctx = i >= n_lat_tiles
    keep_prev = jnp.where((i == 0) | is_ctx, 0.0, 1.0)
    keep_next = jnp.where((i == n_lat_tiles - 1) | is_ctx, 0.0, 1.0)
    _fill_halo_buffer(cbuf, glu_ref, glu_p_ref, glu_n_ref, keep_prev, keep_next)
    _fill_halo_buffer(sbuf, cx_ref, cx_p_ref, cx_n_ref, keep_prev, keep_next)

    x = _tile_rows(x_ref, xc_ref, n_lat_tiles)
    d = x.shape[-1]
    h = _norm_modulate(x, lng_ref[...], mod_ref[0:1, :], mod_ref[1:2, :]).astype(BF16)

    def gate(branch):
        return _sigmoid(_dot(h, wg_ref[:, branch * d:(branch + 1) * d]))

    merged = gate(0) * _dot(_tile_rows(att_ref, attc_ref, n_lat_tiles), wmo_ref[:, 0:d])

    u = _depthwise(cbuf, dw_ref, CONV_WIDTH) + dwb_ref[...]
    mu = jnp.mean(u, axis=-1, keepdims=True)
    uc = u - mu
    var = jnp.mean(uc * uc, axis=-1, keepdims=True)
    u = uc * lax.rsqrt(var + EPS) * cg_ref[...] + cb_ref[...]
    u = u * _sigmoid(u)
    merged = merged + gate(1) * _dot(u.astype(BF16), wco_ref[:, 0:d])

    sc = bg_ref[0] * _depthwise(sbuf, scw_ref, SC_WIDTH)
    merged = merged + gate(2) * _dot(sc.astype(BF16), wso_ref[:, 0:d])

    o_ref[0] = x + mod_ref[2:3, :] * _dot(merged.astype(BF16), wo_ref[:, 0:d])


def _merge(l, xs, atts, mods, glu, cx, bg, lng, wg, wmo, dw, dwb, cg, cb, wco, scw, wso, wo, n_tiles,
           n_lat_tiles):
    b, _, d = xs[0].shape
    per = TM // HALO
    last_halo = glu.shape[1] // HALO - 1
    row = lambda bi, i: (bi, i, 0)
    prev = lambda bi, i: (bi, jnp.maximum(i * per - 1, 0), 0)
    nxt = lambda bi, i: (bi, jnp.minimum((i + 1) * per, last_halo), 0)
    params = (lng, wg, wmo, dw, dwb, cg, cb, wco, scw, wso, wo)
    return pl.pallas_call(
        functools.partial(_merge_kernel, n_lat_tiles=n_lat_tiles, x_split=len(xs) == 2,
                          att_split=len(atts) == 2),
        out_shape=jax.ShapeDtypeStruct((b, n_tiles * TM, d), F32),
        grid=(b, n_tiles),
        in_specs=(
            _stream_specs(xs, n_lat_tiles)
            + _stream_specs(atts, n_lat_tiles)
            + [_mod_spec(l, mods, n_lat_tiles, b),
               pl.BlockSpec((1, TM, CONV_CH), row),
               pl.BlockSpec((1, HALO, CONV_CH), prev),
               pl.BlockSpec((1, HALO, CONV_CH), nxt),
               pl.BlockSpec((1, TM, SC_CH), row),
               pl.BlockSpec((1, HALO, SC_CH), prev),
               pl.BlockSpec((1, HALO, SC_CH), nxt),
               pl.BlockSpec((1, TM, SC_CH), row)]
            + [_layer_spec(l, p) for p in params]),
        out_specs=pl.BlockSpec((1, TM, d), row),
        scratch_shapes=[
            pltpu.VMEM((TM + 2 * HALO, CONV_CH), F32),
            pltpu.VMEM((TM + 2 * HALO, SC_CH), F32),
        ],
        compiler_params=pltpu.CompilerParams(
            dimension_semantics=("arbitrary", "arbitrary"), vmem_limit_bytes=VMEM_LIMIT),
        name="merge",
    )(*xs, *atts, mods, glu, glu, glu, cx, cx, cx, bg, *params)


def _ffn_kernel(x_ref, mod_ref, modc_ref, lng_ref, w1_ref, w3_ref, w2_ref, fg_ref, o_ref, *,
                n_lat_tiles, final_norm):
    per = x_ref.shape[1] // TM
    for sub in range(per):
        rows = slice(sub * TM, (sub + 1) * TM)
        is_ctx = pl.program_id(1) * per + sub >= n_lat_tiles
        mod = jnp.where(is_ctx, modc_ref[...], mod_ref[...])
        x = x_ref[0, rows, :]
        h = _norm_modulate(x, lng_ref[...], mod[3:4, :], mod[4:5, :]).astype(BF16)
        a = _dot(h, w1_ref[...])
        a = a * _sigmoid(a) * _dot(h, w3_ref[...])
        y = x + mod[5:6, :] * _dot(a.astype(BF16), w2_ref[...])
        if final_norm:
            y = _rms(y, fg_ref[...])
        o_ref[0, rows, :] = y


def _ffn(l, xa, mods, lng, w1, w3, w2, fg, n_tiles, n_lat_tiles, final_norm):
    b, _, d = xa.shape
    per = max(p for p in range(1, FFN_TILES + 1) if n_tiles % p == 0)
    row = lambda bi, i: (bi, i, 0)
    n_mod = mods.shape[2]
    return pl.pallas_call(
        functools.partial(_ffn_kernel, n_lat_tiles=n_lat_tiles, final_norm=final_norm),
        out_shape=jax.ShapeDtypeStruct((b, n_tiles * TM, d), F32),
        grid=(b, n_tiles // per),
        in_specs=[
            pl.BlockSpec((1, per * TM, d), row),
            pl.BlockSpec((None, None, n_mod, d), lambda bi, i: (l, bi, 0, 0)),
            pl.BlockSpec((None, None, n_mod, d), lambda bi, i: (l, b, 0, 0)),
            _layer_spec(l, lng),
            _layer_spec(l, w1),
            _layer_spec(l, w3),
            _layer_spec(l, w2),
            pl.BlockSpec((1, d), lambda bi, i: (0, 0)),
        ],
        out_specs=pl.BlockSpec((1, per * TM, d), row),
        compiler_params=pltpu.CompilerParams(
            dimension_semantics=("arbitrary", "arbitrary"), vmem_limit_bytes=VMEM_LIMIT),
        name="ffn",
    )(xa, mods, mods, lng, w1, w3, w2, fg)


def _rope_patterns(n_lat, n_ctx):
    t = np.arange(n_lat)
    half = QK_ROPE // 2
    inv = np.float64(ROPE_THETA) ** (-np.arange(0, half, 2, dtype=np.float64) / half)
    ang_r = (t // GRID_W)[:, None] * inv
    ang_c = (t % GRID_W)[:, None] * inv
    cos = np.concatenate([np.cos(ang_r)] * 2 + [np.cos(ang_c)] * 2, axis=1)
    sin = np.concatenate([-np.sin(ang_r), np.sin(ang_r), -np.sin(ang_c), np.sin(ang_c)], axis=1)
    pad = HEAD_PAD - QK_NOPE - QK_ROPE
    rc = np.concatenate([np.ones((n_lat, QK_NOPE)), cos, np.zeros((n_lat, pad))], axis=1)
    rs = np.concatenate([np.zeros((n_lat, QK_NOPE)), sin, np.zeros((n_lat, pad))], axis=1)
    rc_ctx = np.concatenate([np.ones((n_ctx, QK_NOPE + QK_ROPE)), np.zeros((n_ctx, pad))], axis=1)
    rc = np.concatenate([rc, rc_ctx], axis=0).astype(np.float32)
    rs = np.concatenate([rs, np.zeros((n_ctx, HEAD_PAD))], axis=0).astype(np.float32)
    return rc, rs, np.ascontiguousarray(rc.T), np.ascontiguousarray(rs.T)


def _swap_rope_columns(w):
    idx = jnp.arange(QK_ROPE) ^ (QK_ROPE // 4)
    return w[..., idx]


def _projection_weights(w_in, w_q_b, w_kv_b):
    depth, d, _ = w_in.shape
    w_in, w_q_b, w_kv_b = lax.optimization_barrier(
        tuple(w.astype(BF16) for w in (w_in, w_q_b, w_kv_b)))
    q_end = Q_LORA
    kv_end = q_end + KV_LORA + QK_ROPE
    sc_end = kv_end + 2 * CONV_CH + 3 * SC_CH
    w_kr = w_in[..., q_end + KV_LORA:kv_end]
    assert HEAD_PAD == QK_NOPE + 2 * QK_ROPE
    pad_lo = jnp.zeros((depth, d, QK_NOPE), w_in.dtype)
    wa = jnp.concatenate([
        w_in[..., :q_end + KV_LORA], w_in[..., kv_end:sc_end],
        pad_lo, w_kr, _swap_rope_columns(w_kr)], axis=-1).astype(BF16)
    wg = _pad_cols(w_in[..., sc_end:])

    wq = w_q_b.reshape(depth, Q_LORA, MLA_HEADS, QK_NOPE + QK_ROPE)
    wqm = jnp.concatenate([wq, _swap_rope_columns(wq[..., QK_NOPE:])], axis=-1)
    wqm = wqm.reshape(depth, Q_LORA, MLA_HEADS * HEAD_PAD)

    wkv = w_kv_b.reshape(depth, KV_LORA, MLA_HEADS, QK_NOPE + V_HEAD)
    wk = jnp.concatenate(
        [wkv[..., :QK_NOPE],
         jnp.zeros((depth, KV_LORA, MLA_HEADS, HEAD_PAD - QK_NOPE), w_kv_b.dtype)], axis=-1)
    wk = wk.reshape(depth, KV_LORA, MLA_HEADS * HEAD_PAD).astype(BF16)
    wv = jnp.concatenate(
        [wkv[..., QK_NOPE:],
         jnp.zeros((depth, KV_LORA, MLA_HEADS, V_PAD - V_HEAD), w_kv_b.dtype)], axis=-1)
    wvt = jnp.swapaxes(wv.reshape(depth, KV_LORA, MLA_HEADS * V_PAD), 1, 2).astype(BF16)
    return wa, wg, jnp.swapaxes(wqm, 1, 2).astype(BF16), wk, wvt


def kernel(x, c, ctx, c_ctx, w_mod, b_mod, ln1_g, w_in, q_a_norm_g, w_q_b, kv_a_norm_g, w_kv_b, w_mla_o, conv_dw, conv_dw_b, conv_ln_g, conv_ln_b, w_conv_o, sc_dw, w_sc_o, w_o, ln2_g, w_ff1, w_ff3, w_ff2, final_g):
    b, n_lat, d = x.shape
    n_ctx = ctx.shape[1]
    depth = w_mod.shape[0]
    assert n_ctx == TM and n_lat % TM == 0 and n_lat % KC == 0 and b + 1 <= MOD_ROWS
    n_lat_tiles = n_lat // TM
    n_all_tiles = (n_lat + n_ctx) // TM

    act = jnp.concatenate([c, c_ctx[None, :], jnp.zeros((MOD_ROWS - b - 1, d), F32)], axis=0)
    mods = _modulation(act, w_mod, b_mod).reshape(depth, MOD_ROWS, N_MOD, d)

    rc, rs, rct, rst = _rope_patterns(n_lat, n_ctx)
    vec = lambda v: v.reshape(depth, 1, -1)
    wa, wg, wqmt, wk, wvt = _projection_weights(w_in, w_q_b, w_kv_b)
    wmo, wco, wso, wo = (_pad_cols(w) for w in (w_mla_o, w_conv_o, w_sc_o, w_o))
    w1, w3, w2 = (w.astype(BF16) for w in (w_ff1, w_ff3, w_ff2))

    xs = (x, ctx)
    for l in range(depth):
        last = l == depth - 1
        qt, k, vt, glu, cx, bg = _inproj(
            l, xs, mods, vec(ln1_g), wa, vec(q_a_norm_g), wqmt, vec(kv_a_norm_g), wk, wvt,
            rc, rs, rct, rst, n_all_tiles, n_lat_tiles)
        atts = (_attention(qt, k, vt, 0, n_lat_tiles, 0),)
        if not last:
            atts += (_attention(qt, k, vt, n_lat_tiles, n_all_tiles - n_lat_tiles, n_lat),)
        n_tiles = n_lat_tiles if last else n_all_tiles
        xa = _merge(l, xs, atts, mods, glu, cx, bg, vec(ln1_g), wg, wmo, conv_dw, vec(conv_dw_b),
                    vec(conv_ln_g), vec(conv_ln_b), wco, sc_dw, wso, wo, n_tiles, n_lat_tiles)
        xa = _ffn(l, xa, mods, vec(ln2_g), w1, w3, w2, final_g.reshape(1, -1), n_tiles,
                  n_lat_tiles, last)
        xs = (xa,)
    return xa
```

```python
import functools
import math

import jax
import jax.numpy as jnp
import numpy as np
from jax import lax
from jax.experimental import pallas as pl
from jax.experimental.pallas import tpu as pltpu

F32 = jnp.float32
BF16 = jnp.bfloat16

GRID_W = 64
MLA_HEADS = 8
QK_NOPE = 64
QK_ROPE = 32
V_HEAD = 64
Q_LORA = 384
KV_LORA = 256
CONV_CH = 256
CONV_WIDTH = 31
SC_CH = 256
SC_WIDTH = 3
N_MOD = 6
ROPE_THETA = 10000.0
EPS = 1e-6

LANES = 128
SUBLANES = 8
HEAD_PAD = LANES
TM = 256
HALO = 16
KC = 256
V_PAD = 80
M_INIT = -1e30
ATTN_HEADS = 2
ATTN_Q_TILES = 4
FFN_TILES = 3
MOD_ROWS = 8
MOD_TN = 1536
VMEM_LIMIT = 56 * 1024 * 1024

Q_SCALE = (QK_NOPE + QK_ROPE) ** -0.5 * math.log2(math.e)

A_Q = 0
A_KV = A_Q + Q_LORA
A_GLU_A = A_KV + KV_LORA
A_GLU_G = A_GLU_A + CONV_CH
A_SC_B = A_GLU_G + CONV_CH
A_SC_C = A_SC_B + SC_CH
A_SC_X = A_SC_C + SC_CH
A_KR = A_SC_X + SC_CH
A_COLS = A_KR + HEAD_PAD


def _dot(a, b):
    return jnp.dot(a, b, preferred_element_type=F32)


def _rms(x, g):
    return x * lax.rsqrt(jnp.mean(x * x, axis=-1, keepdims=True) + EPS) * g


def _norm_modulate(x, g, shift, scale):
    return _rms(x, g) * (1.0 + scale) + shift


def _sigmoid(x):
    return 0.5 * jnp.tanh(0.5 * x) + 0.5


def _pad_cols(w):
    return jnp.pad(w, ((0, 0),) * (w.ndim - 1) + ((0, LANES),)).astype(BF16)


def _mod_kernel(act_ref, w_ref, b_ref, o_ref):
    a = act_ref[...]
    a = a * _sigmoid(a)
    o_ref[0] = _dot(a.astype(BF16), w_ref[0].astype(BF16)) + b_ref[0]


def _modulation(act, w_mod, b_mod):
    depth, d, n = w_mod.shape
    return pl.pallas_call(
        _mod_kernel,
        out_shape=jax.ShapeDtypeStruct((depth, MOD_ROWS, n), F32),
        grid=(depth, n // MOD_TN),
        in_specs=[
            pl.BlockSpec((MOD_ROWS, d), lambda l, j: (0, 0)),
            pl.BlockSpec((1, d, MOD_TN), lambda l, j: (l, 0, j)),
            pl.BlockSpec((1, 1, MOD_TN), lambda l, j: (l, 0, j)),
        ],
        out_specs=pl.BlockSpec((1, MOD_ROWS, MOD_TN), lambda l, j: (l, 0, j)),
        compiler_params=pltpu.CompilerParams(
            dimension_semantics=("arbitrary", "arbitrary"), vmem_limit_bytes=VMEM_LIMIT),
        name="modulation",
    )(act, w_mod, b_mod.reshape(depth, 1, n))


def _dot_t(a, b):
    return lax.dot_general(a, b, (((1,), (1,)), ((), ())), preferred_element_type=F32)


def _tile_rows(lat_ref, ctx_ref, n_lat_tiles):
    if ctx_ref is None:
        return lat_ref[0]
    return jnp.where(pl.program_id(1) >= n_lat_tiles, ctx_ref[0], lat_ref[0])


def _inproj_kernel(*refs, n_lat_tiles, split):
    x_ref, ctx_ref = (refs[0], refs[1]) if split else (refs[0], None)
    (mod_ref, lng_ref, wa_ref, qg_ref, wqmt_ref, kvg_ref, wk_ref, wvt_ref, rc_ref, rs_ref,
     rct_ref, rst_ref, qt_ref, k_ref, vt_ref, glu_ref, cx_ref, bg_ref) = refs[2 if split else 1:]
    x = _tile_rows(x_ref, ctx_ref, n_lat_tiles)
    h = _norm_modulate(x, lng_ref[...], mod_ref[0:1, :], mod_ref[1:2, :])
    proj = _dot(h.astype(BF16), wa_ref[...])
    rope = slice(QK_NOPE, QK_NOPE + QK_ROPE)
    swap = slice(QK_NOPE + QK_ROPE, HEAD_PAD)

    qn = _rms(proj[:, A_Q:A_Q + Q_LORA], qg_ref[...] * Q_SCALE).astype(BF16)
    qm = _dot_t(wqmt_ref[...], qn)
    cos_t, sin_t = rct_ref[rope, :], rst_ref[rope, :]
    pieces = []
    for hh in range(MLA_HEADS):
        head = qm[hh * HEAD_PAD:(hh + 1) * HEAD_PAD]
        pieces += [head[0:QK_NOPE], head[rope] * cos_t + head[swap] * sin_t,
                   jnp.zeros((HEAD_PAD - QK_NOPE - QK_ROPE, TM), F32)]
    qt_ref[0] = jnp.concatenate(pieces, axis=0).astype(BF16)

    cn = _rms(proj[:, A_KV:A_KV + KV_LORA], kvg_ref[...]).astype(BF16)
    kr = proj[:, A_KR:A_KR + HEAD_PAD]
    kr = kr * rc_ref[...] + pltpu.roll(kr, HEAD_PAD - QK_ROPE, axis=1) * rs_ref[...]
    k_ref[0] = (_dot(cn, wk_ref[...]) + jnp.concatenate([kr] * MLA_HEADS, axis=1)).astype(BF16)

    vt = _dot_t(wvt_ref[...], cn)
    rowid = lax.broadcasted_iota(jnp.int32, vt.shape, 0)
    vt_ref[0] = jnp.where(rowid % V_PAD == V_HEAD, 1.0, vt).astype(BF16)

    glu_ref[0] = proj[:, A_GLU_A:A_GLU_A + CONV_CH] * _sigmoid(proj[:, A_GLU_G:A_GLU_G + CONV_CH])
    bg_ref[0] = proj[:, A_SC_B:A_SC_B + SC_CH]
    cx_ref[0] = proj[:, A_SC_C:A_SC_C + SC_CH] * proj[:, A_SC_X:A_SC_X + SC_CH]


def _stream_specs(arrs, n_lat_tiles):
    width = arrs[0].shape[-1]
    if len(arrs) == 1:
        return [pl.BlockSpec((1, TM, width), lambda bi, i: (bi, i, 0))]
    assert arrs[1].shape[1] == TM
    return [pl.BlockSpec((1, TM, width), lambda bi, i: (bi, jnp.minimum(i, n_lat_tiles - 1), 0)),
            pl.BlockSpec((1, TM, width), lambda bi, i: (bi, 0, 0))]


def _layer_spec(l, arr):
    zeros = (0,) * (arr.ndim - 1)
    return pl.BlockSpec((None,) + arr.shape[1:], lambda bi, i: (l,) + zeros)


def _mod_spec(l, mods, n_lat_tiles, ctx_row):
    _, _, n_mod, d = mods.shape
    return pl.BlockSpec((None, None, n_mod, d),
                        lambda bi, i: (l, jnp.where(i < n_lat_tiles, bi, ctx_row), 0, 0))


def _inproj(l, xs, mods, lng, wa, qg, wqmt, kvg, wk, wvt, rc, rs, rct, rst, n_tiles,
            n_lat_tiles):
    b, _, d = xs[0].shape
    t_all = n_tiles * TM
    hq = MLA_HEADS * HEAD_PAD
    hv = MLA_HEADS * V_PAD
    row = lambda bi, i: (bi, i, 0)
    col = lambda bi, i: (bi, 0, i)
    params = (lng, wa, qg, wqmt, kvg, wk, wvt)
    return pl.pallas_call(
        functools.partial(_inproj_kernel, n_lat_tiles=n_lat_tiles, split=len(xs) == 2),
        out_shape=(
            jax.ShapeDtypeStruct((b, hq, t_all), BF16),
            jax.ShapeDtypeStruct((b, t_all, hq), BF16),
            jax.ShapeDtypeStruct((b, hv, t_all), BF16),
            jax.ShapeDtypeStruct((b, t_all, CONV_CH), F32),
            jax.ShapeDtypeStruct((b, t_all, SC_CH), F32),
            jax.ShapeDtypeStruct((b, t_all, SC_CH), F32),
        ),
        grid=(b, n_tiles),
        in_specs=(
            _stream_specs(xs, n_lat_tiles)
            + [_mod_spec(l, mods, n_lat_tiles, b)]
            + [_layer_spec(l, p) for p in params]
            + [pl.BlockSpec((TM, HEAD_PAD), lambda bi, i: (i, 0)),
               pl.BlockSpec((TM, HEAD_PAD), lambda bi, i: (i, 0)),
               pl.BlockSpec((HEAD_PAD, TM), lambda bi, i: (0, i)),
               pl.BlockSpec((HEAD_PAD, TM), lambda bi, i: (0, i))]),
        out_specs=(
            pl.BlockSpec((1, hq, TM), col),
            pl.BlockSpec((1, TM, hq), row),
            pl.BlockSpec((1, hv, TM), col),
            pl.BlockSpec((1, TM, CONV_CH), row),
            pl.BlockSpec((1, TM, SC_CH), row),
            pl.BlockSpec((1, TM, SC_CH), row),
        ),
        compiler_params=pltpu.CompilerParams(
            dimension_semantics=("arbitrary", "arbitrary"), vmem_limit_bytes=VMEM_LIMIT),
        name="inproj",
    )(*xs, mods, *params, rc, rs, rct, rst)


def _attn_kernel(qt_ref, k_ref, vt_ref, o_ref):
    for sub in range(qt_ref.shape[2] // TM):
        _attn_tile(qt_ref, k_ref, vt_ref, o_ref, sub * TM)


def _attn_tile(qt_ref, k_ref, vt_ref, o_ref, q0):
    n_chunks = k_ref.shape[1] // KC
    heads = range(ATTN_HEADS)
    qts = [qt_ref[0, hh * HEAD_PAD:(hh + 1) * HEAD_PAD, q0:q0 + TM] for hh in heads]

    def scores(kb):
        start = pl.multiple_of(kb * KC, KC)
        return tuple(_dot(k_ref[0, pl.ds(start, KC), hh * HEAD_PAD:(hh + 1) * HEAD_PAD], qts[hh])
                     for hh in heads)

    def softmax(sts, ms):
        out = []
        for hh in heads:
            m_new = jnp.maximum(ms[hh], jnp.max(sts[hh], axis=0, keepdims=True))
            out.append((m_new, jnp.exp2(ms[hh] - m_new), jnp.exp2(sts[hh] - m_new).astype(BF16)))
        return tuple(o[0] for o in out), tuple(o[1:] for o in out)

    def accumulate(kb, pts, accs):
        start = pl.multiple_of(jnp.maximum(kb, 0) * KC, KC)
        return tuple(
            pts[hh][0] * accs[hh] + _dot(vt_ref[0, hh * V_PAD:(hh + 1) * V_PAD, pl.ds(start, KC)],
                                         pts[hh][1])
            for hh in heads)

    def step(kb, carry):
        sts, ms, pts, accs = carry
        ms_new, pts_new = softmax(sts, ms)
        return scores(kb), ms_new, pts_new, accumulate(kb - 2, pts, accs)

    ms = tuple(jnp.full((1, TM), M_INIT, F32) for _ in heads)
    accs = tuple(jnp.zeros((V_PAD, TM), F32) for _ in heads)
    pts = tuple((jnp.ones((1, TM), F32), jnp.zeros((KC, TM), BF16)) for _ in heads)
    carry = (scores(0), ms, pts, accs)
    if n_chunks > 1:
        carry = lax.fori_loop(1, n_chunks, step, carry, unroll=n_chunks - 1)
    sts, ms, pts, accs = carry
    accs = accumulate(n_chunks - 2, pts, accs)
    _, pts = softmax(sts, ms)
    accs = accumulate(n_chunks - 1, pts, accs)
    outs = [(acc[0:V_HEAD] / acc[V_HEAD:V_HEAD + 1]).T for acc in accs]
    o_ref[0, q0:q0 + TM, :] = jnp.concatenate(outs, axis=1).astype(BF16)


def _attention(qt, k, vt, q_tile_lo, n_q_tiles, first_key):
    b, t_all, _ = k.shape
    groups = MLA_HEADS // ATTN_HEADS
    per = ATTN_Q_TILES if n_q_tiles % ATTN_Q_TILES == 0 and q_tile_lo % ATTN_Q_TILES == 0 else 1
    tq = per * TM
    lo = q_tile_lo // per
    n_keys = t_all - first_key
    assert first_key % n_keys == 0
    key_blk = first_key // n_keys
    return pl.pallas_call(
        _attn_kernel,
        out_shape=jax.ShapeDtypeStruct((b, n_q_tiles * TM, MLA_HEADS * V_HEAD), BF16),
        grid=(b, groups, n_q_tiles // per),
        in_specs=[
            pl.BlockSpec((1, ATTN_HEADS * HEAD_PAD, tq), lambda bi, j, i: (bi, j, i + lo)),
            pl.BlockSpec((1, n_keys, ATTN_HEADS * HEAD_PAD), lambda bi, j, i: (bi, key_blk, j)),
            pl.BlockSpec((1, ATTN_HEADS * V_PAD, n_keys), lambda bi, j, i: (bi, j, key_blk)),
        ],
        out_specs=pl.BlockSpec((1, tq, ATTN_HEADS * V_HEAD), lambda bi, j, i: (bi, i, j)),
        compiler_params=pltpu.CompilerParams(
            dimension_semantics=("arbitrary", "arbitrary", "arbitrary"), vmem_limit_bytes=VMEM_LIMIT),
        name="attention",
    )(qt, k, vt)


def _fill_halo_buffer(buf, cur_ref, prev_ref, next_ref, keep_prev, keep_next):
    buf[0:HALO, :] = prev_ref[0] * keep_prev
    buf[HALO:HALO + TM, :] = cur_ref[0]
    buf[HALO + TM:HALO + TM + HALO, :] = next_ref[0] * keep_next


def _depthwise(buf, w_ref, width):
    base = HALO - (width - 1) // 2
    out = None
    for r in range(SUBLANES):
        acc = None
        for k in range(width):
            if (base + k) % SUBLANES != r:
                continue
            off = base + k - r
            term = w_ref[k:k + 1, :] * buf[off:off + TM + SUBLANES, :]
            acc = term if acc is None else acc + term
        if acc is not None:
            out = acc[r:r + TM, :] if out is None else out + acc[r:r + TM, :]
    return out


def _merge_kernel(*refs, n_lat_tiles, x_split, att_split):
    refs = list(refs)
    x_ref = refs.pop(0)
    xc_ref = refs.pop(0) if x_split else None
    att_ref = refs.pop(0)
    attc_ref = refs.pop(0) if att_split else None
    (mod_ref, glu_ref, glu_p_ref, glu_n_ref, cx_ref, cx_p_ref, cx_n_ref, bg_ref, lng_ref, wg_ref,
     wmo_ref, dw_ref, dwb_ref, cg_ref, cb_ref, wco_ref, scw_ref, wso_ref, wo_ref, o_ref, cbuf,
     sbuf) = refs
    i = pl.program_id(1)
    is_ctx = i >= n_lat_tiles
    keep_prev = jnp.where((i == 0) | is_ctx, 0.0, 1.0)
    keep_next = jnp.where((i == n_lat_tiles - 1) | is_ctx, 0.0, 1.0)
    _fill_halo_buffer(cbuf, glu_ref, glu_p_ref, glu_n_ref, keep_prev, keep_next)
    _fill_halo_buffer(sbuf, cx_ref, cx_p_ref, cx_n_ref, keep_prev, keep_next)

    x = _tile_rows(x_ref, xc_ref, n_lat_tiles)
    d = x.shape[-1]
    h = _norm_modulate(x, lng_ref[...], mod_ref[0:1, :], mod_ref[1:2, :]).astype(BF16)

    def gate(branch):
        return _sigmoid(_dot(h, wg_ref[:, branch * d:(branch + 1) * d]))

    merged = gate(0) * _dot(_tile_rows(att_ref, attc_ref, n_lat_tiles), wmo_ref[:, 0:d])

    u = _depthwise(cbuf, dw_ref, CONV_WIDTH) + dwb_ref[...]
    mu = jnp.mean(u, axis=-1, keepdims=True)
    uc = u - mu
    var = jnp.mean(uc * uc, axis=-1, keepdims=True)
    u = uc * lax.rsqrt(var + EPS) * cg_ref[...] + cb_ref[...]
    u = u * _sigmoid(u)
    merged = merged + gate(1) * _dot(u.astype(BF16), wco_ref[:, 0:d])

    sc = bg_ref[0] * _depthwise(sbuf, scw_ref, SC_WIDTH)
    merged = merged + gate(2) * _dot(sc.astype(BF16), wso_ref[:, 0:d])

    o_ref[0] = x + mod_ref[2:3, :] * _dot(merged.astype(BF16), wo_ref[:, 0:d])


def _merge(l, xs, atts, mods, glu, cx, bg, lng, wg, wmo, dw, dwb, cg, cb, wco, scw, wso, wo, n_tiles,
           n_lat_tiles):
    b, _, d = xs[0].shape
    per = TM // HALO
    last_halo = glu.shape[1] // HALO - 1
    row = lambda bi, i: (bi, i, 0)
    prev = lambda bi, i: (bi, jnp.maximum(i * per - 1, 0), 0)
    nxt = lambda bi, i: (bi, jnp.minimum((i + 1) * per, last_halo), 0)
    params = (lng, wg, wmo, dw, dwb, cg, cb, wco, scw, wso, wo)
    return pl.pallas_call(
        functools.partial(_merge_kernel, n_lat_tiles=n_lat_tiles, x_split=len(xs) == 2,
                          att_split=len(atts) == 2),
        out_shape=jax.ShapeDtypeStruct((b, n_tiles * TM, d), F32),
        grid=(b, n_tiles),
        in_specs=(
            _stream_specs(xs, n_lat_tiles)
            + _stream_specs(atts, n_lat_tiles)
            + [_mod_spec(l, mods, n_lat_tiles, b),
               pl.BlockSpec((1, TM, CONV_CH), row),
               pl.BlockSpec((1, HALO, CONV_CH), prev),
               pl.BlockSpec((1, HALO, CONV_CH), nxt),
               pl.BlockSpec((1, TM, SC_CH), row),
               pl.BlockSpec((1, HALO, SC_CH), prev),
               pl.BlockSpec((1, HALO, SC_CH), nxt),
               pl.BlockSpec((1, TM, SC_CH), row)]
            + [_layer_spec(l, p) for p in params]),
        out_specs=pl.BlockSpec((1, TM, d), row),
        scratch_shapes=[
            pltpu.VMEM((TM + 2 * HALO, CONV_CH), F32),
            pltpu.VMEM((TM + 2 * HALO, SC_CH), F32),
        ],
        compiler_params=pltpu.CompilerParams(
            dimension_semantics=("arbitrary", "arbitrary"), vmem_limit_bytes=VMEM_LIMIT),
        name="merge",
    )(*xs, *atts, mods, glu, glu, glu, cx, cx, cx, bg, *params)


def _ffn_kernel(x_ref, mod_ref, modc_ref, lng_ref, w1_ref, w3_ref, w2_ref, fg_ref, o_ref, *,
                n_lat_tiles, final_norm):
    per = x_ref.shape[1] // TM
    for sub in range(per):
        rows = slice(sub * TM, (sub + 1) * TM)
        is_ctx = pl.program_id(1) * per + sub >= n_lat_tiles
        mod = jnp.where(is_ctx, modc_ref[...], mod_ref[...])
        x = x_ref[0, rows, :]
        h = _norm_modulate(x, lng_ref[...], mod[3:4, :], mod[4:5, :]).astype(BF16)
        a = _dot(h, w1_ref[...])
        a = a * _sigmoid(a) * _dot(h, w3_ref[...])
        y = x + mod[5:6, :] * _dot(a.astype(BF16), w2_ref[...])
        if final_norm:
            y = _rms(y, fg_ref[...])
        o_ref[0, rows, :] = y


def _ffn(l, xa, mods, lng, w1, w3, w2, fg, n_tiles, n_lat_tiles, final_norm):
    b, _, d = xa.shape
    per = max(p for p in range(1, FFN_TILES + 1) if n_tiles % p == 0)
    row = lambda bi, i: (bi, i, 0)
    n_mod = mods.shape[2]
    return pl.pallas_call(
        functools.partial(_ffn_kernel, n_lat_tiles=n_lat_tiles, final_norm=final_norm),
        out_shape=jax.ShapeDtypeStruct((b, n_tiles * TM, d), F32),
        grid=(b, n_tiles // per),
        in_specs=[
            pl.BlockSpec((1, per * TM, d), row),
            pl.BlockSpec((None, None, n_mod, d), lambda bi, i: (l, bi, 0, 0)),
            pl.BlockSpec((None, None, n_mod, d), lambda bi, i: (l, b, 0, 0)),
            _layer_spec(l, lng),
            _layer_spec(l, w1),
            _layer_spec(l, w3),
            _layer_spec(l, w2),
            pl.BlockSpec((1, d), lambda bi, i: (0, 0)),
        ],
        out_specs=pl.BlockSpec((1, per * TM, d), row),
        compiler_params=pltpu.CompilerParams(
            dimension_semantics=("arbitrary", "arbitrary"), vmem_limit_bytes=VMEM_LIMIT),
        name="ffn",
    )(xa, mods, mods, lng, w1, w3, w2, fg)


def _rope_patterns(n_lat, n_ctx):
    t = np.arange(n_lat)
    half = QK_ROPE // 2
    inv = np.float64(ROPE_THETA) ** (-np.arange(0, half, 2, dtype=np.float64) / half)
    ang_r = (t // GRID_W)[:, None] * inv
    ang_c = (t % GRID_W)[:, None] * inv
    cos = np.concatenate([np.cos(ang_r)] * 2 + [np.cos(ang_c)] * 2, axis=1)
    sin = np.concatenate([-np.sin(ang_r), np.sin(ang_r), -np.sin(ang_c), np.sin(ang_c)], axis=1)
    pad = HEAD_PAD - QK_NOPE - QK_ROPE
    rc = np.concatenate([np.ones((n_lat, QK_NOPE)), cos, np.zeros((n_lat, pad))], axis=1)
    rs = np.concatenate([np.zeros((n_lat, QK_NOPE)), sin, np.zeros((n_lat, pad))], axis=1)
    rc_ctx = np.concatenate([np.ones((n_ctx, QK_NOPE + QK_ROPE)), np.zeros((n_ctx, pad))], axis=1)
    rc = np.concatenate([rc, rc_ctx], axis=0).astype(np.float32)
    rs = np.concatenate([rs, np.zeros((n_ctx, HEAD_PAD))], axis=0).astype(np.float32)
    return rc, rs, np.ascontiguousarray(rc.T), np.ascontiguousarray(rs.T)


def _swap_rope_columns(w):
    idx = jnp.arange(QK_ROPE) ^ (QK_ROPE // 4)
    return w[..., idx]


def _projection_weights(w_in, w_q_b, w_kv_b):
    depth, d, _ = w_in.shape
    w_in, w_q_b, w_kv_b = (w.astype(BF16) for w in (w_in, w_q_b, w_kv_b))
    q_end = Q_LORA
    kv_end = q_end + KV_LORA + QK_ROPE
    sc_end = kv_end + 2 * CONV_CH + 3 * SC_CH
    w_kr = w_in[..., q_end + KV_LORA:kv_end]
    assert HEAD_PAD == QK_NOPE + 2 * QK_ROPE
    pad_lo = jnp.zeros((depth, d, QK_NOPE), w_in.dtype)
    wa = jnp.concatenate([
        w_in[..., :q_end + KV_LORA], w_in[..., kv_end:sc_end],
        pad_lo, w_kr, _swap_rope_columns(w_kr)], axis=-1).astype(BF16)
    wg = _pad_cols(w_in[..., sc_end:])

    wq = w_q_b.reshape(depth, Q_LORA, MLA_HEADS, QK_NOPE + QK_ROPE)
    wqm = jnp.concatenate([wq, _swap_rope_columns(wq[..., QK_NOPE:])], axis=-1)
    wqm = wqm.reshape(depth, Q_LORA, MLA_HEADS * HEAD_PAD)

    wkv = w_kv_b.reshape(depth, KV_LORA, MLA_HEADS, QK_NOPE + V_HEAD)
    wk = jnp.concatenate(
        [wkv[..., :QK_NOPE],
         jnp.zeros((depth, KV_LORA, MLA_HEADS, HEAD_PAD - QK_NOPE), w_kv_b.dtype)], axis=-1)
    wk = wk.reshape(depth, KV_LORA, MLA_HEADS * HEAD_PAD).astype(BF16)
    wv = jnp.concatenate(
        [wkv[..., QK_NOPE:],
         jnp.zeros((depth, KV_LORA, MLA_HEADS, V_PAD - V_HEAD), w_kv_b.dtype)], axis=-1)
    wvt = jnp.swapaxes(wv.reshape(depth, KV_LORA, MLA_HEADS * V_PAD), 1, 2).astype(BF16)
    return wa, wg, jnp.swapaxes(wqm, 1, 2).astype(BF16), wk, wvt


def kernel(x, c, ctx, c_ctx, w_mod, b_mod, ln1_g, w_in, q_a_norm_g, w_q_b, kv_a_norm_g, w_kv_b, w_mla_o, conv_dw, conv_dw_b, conv_ln_g, conv_ln_b, w_conv_o, sc_dw, w_sc_o, w_o, ln2_g, w_ff1, w_ff3, w_ff2, final_g):
    b, n_lat, d = x.shape
    n_ctx = ctx.shape[1]
    depth = w_mod.shape[0]
    assert n_ctx == TM and n_lat % TM == 0 and n_lat % KC == 0 and n_ctx % KC == 0
    assert b + 1 <= MOD_ROWS
    n_lat_tiles = n_lat // TM
    n_all_tiles = (n_lat + n_ctx) // TM

    act = jnp.concatenate([c, c_ctx[None, :], jnp.zeros((MOD_ROWS - b - 1, d), F32)], axis=0)
    mods = _modulation(act, w_mod, b_mod).reshape(depth, MOD_ROWS, N_MOD, d)

    rc, rs, rct, rst = _rope_patterns(n_lat, n_ctx)
    vec = lambda v: v.reshape(depth, 1, -1)
    wa, wg, wqmt, wk, wvt = _projection_weights(w_in, w_q_b, w_kv_b)
    wmo, wco, wso, wo = (_pad_cols(w) for w in (w_mla_o, w_conv_o, w_sc_o, w_o))
    w1, w3, w2 = (w.astype(BF16) for w in (w_ff1, w_ff3, w_ff2))

    xs = (x, ctx)
    for l in range(depth):
        last = l == depth - 1
        qt, k, vt, glu, cx, bg = _inproj(
            l, xs, mods, vec(ln1_g), wa, vec(q_a_norm_g), wqmt, vec(kv_a_norm_g), wk, wvt,
            rc, rs, rct, rst, n_all_tiles, n_lat_tiles)
        atts = (_attention(qt, k, vt, 0, n_lat_tiles, 0),)
        if not last:
            atts += (_attention(qt, k, vt, n_lat_tiles, n_all_tiles - n_lat_tiles, n_lat),)
        n_tiles = n_lat_tiles if last else n_all_tiles
        xa = _merge(l, xs, atts, mods, glu, cx, bg, vec(ln1_g), wg, wmo, conv_dw, vec(conv_dw_b),
                    vec(conv_ln_g), vec(conv_ln_b), wco, sc_dw, wso, wo, n_tiles, n_lat_tiles)
        xa = _ffn(l, xa, mods, vec(ln2_g), w1, w3, w2, final_g.reshape(1, -1), n_tiles,
                  n_lat_tiles, last)
        xs = (xa,)
    return xa
```

```python
import functools
import math

import jax
import jax.numpy as jnp
import numpy as np
from jax import lax
from jax.experimental import pallas as pl
from jax.experimental.pallas import tpu as pltpu

F32 = jnp.float32
BF16 = jnp.bfloat16

GRID_W = 64
MLA_HEADS = 8
QK_NOPE = 64
QK_ROPE = 32
V_HEAD = 64
Q_LORA = 384
KV_LORA = 256
CONV_CH = 256
CONV_WIDTH = 31
SC_CH = 256
SC_WIDTH = 3
N_MOD = 6
ROPE_THETA = 10000.0
EPS = 1e-6

LANES = 128
SUBLANES = 8
HEAD_PAD = LANES
TM = 256
HALO = 16
KC = 256
V_PAD = 80
M_INIT = -1e30
ATTN_HEADS = 2
ATTN_Q_TILES = 4
FFN_TILES = 3
MOD_ROWS = 8
MOD_TN = 1536
VMEM_LIMIT = 56 * 1024 * 1024

Q_SCALE = (QK_NOPE + QK_ROPE) ** -0.5 * math.log2(math.e)

A_Q = 0
A_KV = A_Q + Q_LORA
A_GLU_A = A_KV + KV_LORA
A_GLU_G = A_GLU_A + CONV_CH
A_SC_B = A_GLU_G + CONV_CH
A_SC_C = A_SC_B + SC_CH
A_SC_X = A_SC_C + SC_CH
A_KR = A_SC_X + SC_CH
A_COLS = A_KR + HEAD_PAD


def _dot(a, b):
    return jnp.dot(a, b, preferred_element_type=F32)


def _rms(x, g):
    return x * lax.rsqrt(jnp.mean(x * x, axis=-1, keepdims=True) + EPS) * g


def _norm_modulate(x, g, shift, scale):
    return _rms(x, g) * (1.0 + scale) + shift


def _sigmoid(x):
    return 0.5 * jnp.tanh(0.5 * x) + 0.5


def _pad_cols(w):
    return jnp.pad(w, ((0, 0),) * (w.ndim - 1) + ((0, LANES),)).astype(BF16)


def _mod_kernel(act_ref, w_ref, b_ref, o_ref):
    a = act_ref[...]
    a = a * _sigmoid(a)
    o_ref[0] = _dot(a.astype(BF16), w_ref[0].astype(BF16)) + b_ref[0]


def _modulation(act, w_mod, b_mod):
    depth, d, n = w_mod.shape
    return pl.pallas_call(
        _mod_kernel,
        out_shape=jax.ShapeDtypeStruct((depth, MOD_ROWS, n), F32),
        grid=(depth, n // MOD_TN),
        in_specs=[
            pl.BlockSpec((MOD_ROWS, d), lambda l, j: (0, 0)),
            pl.BlockSpec((1, d, MOD_TN), lambda l, j: (l, 0, j)),
            pl.BlockSpec((1, 1, MOD_TN), lambda l, j: (l, 0, j)),
        ],
        out_specs=pl.BlockSpec((1, MOD_ROWS, MOD_TN), lambda l, j: (l, 0, j)),
        compiler_params=pltpu.CompilerParams(
            dimension_semantics=("arbitrary", "arbitrary"), vmem_limit_bytes=VMEM_LIMIT),
        name="modulation",
    )(act, w_mod, b_mod.reshape(depth, 1, n))


def _dot_t(a, b):
    return lax.dot_general(a, b, (((1,), (1,)), ((), ())), preferred_element_type=F32)


def _tile_rows(lat_ref, ctx_ref, n_lat_tiles):
    if ctx_ref is None:
        return lat_ref[0]
    return jnp.where(pl.program_id(1) >= n_lat_tiles, ctx_ref[0], lat_ref[0])


def _inproj_kernel(*refs, n_lat_tiles, split):
    x_ref, ctx_ref = (refs[0], refs[1]) if split else (refs[0], None)
    (mod_ref, lng_ref, wa_ref, qg_ref, wqmt_ref, kvg_ref, wk_ref, wvt_ref, rc_ref, rs_ref,
     rct_ref, rst_ref, qt_ref, k_ref, vt_ref, glu_ref, cx_ref, bg_ref) = refs[2 if split else 1:]
    x = _tile_rows(x_ref, ctx_ref, n_lat_tiles)
    h = _norm_modulate(x, lng_ref[...], mod_ref[0:1, :], mod_ref[1:2, :])
    proj = _dot(h.astype(BF16), wa_ref[...])
    rope = slice(QK_NOPE, QK_NOPE + QK_ROPE)
    swap = slice(QK_NOPE + QK_ROPE, HEAD_PAD)

    qn = _rms(proj[:, A_Q:A_Q + Q_LORA], qg_ref[...] * Q_SCALE).astype(BF16)
    qm = _dot_t(wqmt_ref[...], qn)
    cos_t, sin_t = rct_ref[rope, :], rst_ref[rope, :]
    pieces = []
    for hh in range(MLA_HEADS):
        head = qm[hh * HEAD_PAD:(hh + 1) * HEAD_PAD]
        pieces += [head[0:QK_NOPE], head[rope] * cos_t + head[swap] * sin_t,
                   jnp.zeros((HEAD_PAD - QK_NOPE - QK_ROPE, TM), F32)]
    qt_ref[0] = jnp.concatenate(pieces, axis=0).astype(BF16)

    cn = _rms(proj[:, A_KV:A_KV + KV_LORA], kvg_ref[...]).astype(BF16)
    kr = proj[:, A_KR:A_KR + HEAD_PAD]
    kr = kr * rc_ref[...] + pltpu.roll(kr, HEAD_PAD - QK_ROPE, axis=1) * rs_ref[...]
    k_ref[0] = (_dot(cn, wk_ref[...]) + jnp.concatenate([kr] * MLA_HEADS, axis=1)).astype(BF16)

    vt = _dot_t(wvt_ref[...], cn)
    rowid = lax.broadcasted_iota(jnp.int32, vt.shape, 0)
    vt_ref[0] = jnp.where(rowid % V_PAD == V_HEAD, 1.0, vt).astype(BF16)

    glu_ref[0] = proj[:, A_GLU_A:A_GLU_A + CONV_CH] * _sigmoid(proj[:, A_GLU_G:A_GLU_G + CONV_CH])
    bg_ref[0] = proj[:, A_SC_B:A_SC_B + SC_CH]
    cx_ref[0] = proj[:, A_SC_C:A_SC_C + SC_CH] * proj[:, A_SC_X:A_SC_X + SC_CH]


def _stream_specs(arrs, n_lat_tiles):
    width = arrs[0].shape[-1]
    if len(arrs) == 1:
        return [pl.BlockSpec((1, TM, width), lambda bi, i: (bi, i, 0))]
    assert arrs[1].shape[1] == TM
    return [pl.BlockSpec((1, TM, width), lambda bi, i: (bi, jnp.minimum(i, n_lat_tiles - 1), 0)),
            pl.BlockSpec((1, TM, width), lambda bi, i: (bi, 0, 0))]


def _stream_specs_t(arrs, n_lat_tiles):
    feat = arrs[0].shape[1]
    if len(arrs) == 1:
        return [pl.BlockSpec((1, feat, TM), lambda bi, i: (bi, 0, i))]
    assert arrs[1].shape[2] == TM
    return [pl.BlockSpec((1, feat, TM), lambda bi, i: (bi, 0, jnp.minimum(i, n_lat_tiles - 1))),
            pl.BlockSpec((1, feat, TM), lambda bi, i: (bi, 0, 0))]


def _layer_spec(l, arr):
    zeros = (0,) * (arr.ndim - 1)
    return pl.BlockSpec((None,) + arr.shape[1:], lambda bi, i: (l,) + zeros)


def _mod_spec(l, mods, n_lat_tiles, ctx_row):
    _, _, n_mod, d = mods.shape
    return pl.BlockSpec((None, None, n_mod, d),
                        lambda bi, i: (l, jnp.where(i < n_lat_tiles, bi, ctx_row), 0, 0))


def _inproj(l, xs, mods, lng, wa, qg, wqmt, kvg, wk, wvt, rc, rs, rct, rst, n_tiles,
            n_lat_tiles):
    b, _, d = xs[0].shape
    t_all = n_tiles * TM
    hq = MLA_HEADS * HEAD_PAD
    hv = MLA_HEADS * V_PAD
    row = lambda bi, i: (bi, i, 0)
    col = lambda bi, i: (bi, 0, i)
    params = (lng, wa, qg, wqmt, kvg, wk, wvt)
    return pl.pallas_call(
        functools.partial(_inproj_kernel, n_lat_tiles=n_lat_tiles, split=len(xs) == 2),
        out_shape=(
            jax.ShapeDtypeStruct((b, hq, t_all), BF16),
            jax.ShapeDtypeStruct((b, t_all, hq), BF16),
            jax.ShapeDtypeStruct((b, hv, t_all), BF16),
            jax.ShapeDtypeStruct((b, t_all, CONV_CH), F32),
            jax.ShapeDtypeStruct((b, t_all, SC_CH), F32),
            jax.ShapeDtypeStruct((b, t_all, SC_CH), F32),
        ),
        grid=(b, n_tiles),
        in_specs=(
            _stream_specs(xs, n_lat_tiles)
            + [_mod_spec(l, mods, n_lat_tiles, b)]
            + [_layer_spec(l, p) for p in params]
            + [pl.BlockSpec((TM, HEAD_PAD), lambda bi, i: (i, 0)),
               pl.BlockSpec((TM, HEAD_PAD), lambda bi, i: (i, 0)),
               pl.BlockSpec((HEAD_PAD, TM), lambda bi, i: (0, i)),
               pl.BlockSpec((HEAD_PAD, TM), lambda bi, i: (0, i))]),
        out_specs=(
            pl.BlockSpec((1, hq, TM), col),
            pl.BlockSpec((1, TM, hq), row),
            pl.BlockSpec((1, hv, TM), col),
            pl.BlockSpec((1, TM, CONV_CH), row),
            pl.BlockSpec((1, TM, SC_CH), row),
            pl.BlockSpec((1, TM, SC_CH), row),
        ),
        compiler_params=pltpu.CompilerParams(
            dimension_semantics=("arbitrary", "arbitrary"), vmem_limit_bytes=VMEM_LIMIT),
        name="inproj",
    )(*xs, mods, *params, rc, rs, rct, rst)


def _attn_kernel(qt_ref, k_ref, vt_ref, o_ref):
    for sub in range(qt_ref.shape[2] // TM):
        _attn_tile(qt_ref, k_ref, vt_ref, o_ref, sub * TM)


def _attn_tile(qt_ref, k_ref, vt_ref, o_ref, q0):
    n_chunks = k_ref.shape[1] // KC
    heads = range(ATTN_HEADS)
    qts = [qt_ref[0, hh * HEAD_PAD:(hh + 1) * HEAD_PAD, q0:q0 + TM] for hh in heads]

    def scores(kb):
        start = pl.multiple_of(kb * KC, KC)
        return tuple(_dot(k_ref[0, pl.ds(start, KC), hh * HEAD_PAD:(hh + 1) * HEAD_PAD], qts[hh])
                     for hh in heads)

    def softmax(sts, ms):
        out = []
        for hh in heads:
            m_new = jnp.maximum(ms[hh], jnp.max(sts[hh], axis=0, keepdims=True))
            out.append((m_new, jnp.exp2(ms[hh] - m_new), jnp.exp2(sts[hh] - m_new).astype(BF16)))
        return tuple(o[0] for o in out), tuple(o[1:] for o in out)

    def accumulate(kb, pts, accs):
        start = pl.multiple_of(jnp.maximum(kb, 0) * KC, KC)
        return tuple(
            pts[hh][0] * accs[hh] + _dot(vt_ref[0, hh * V_PAD:(hh + 1) * V_PAD, pl.ds(start, KC)],
                                         pts[hh][1])
            for hh in heads)

    def step(kb, carry):
        sts, ms, pts, accs = carry
        ms_new, pts_new = softmax(sts, ms)
        return scores(kb), ms_new, pts_new, accumulate(kb - 2, pts, accs)

    ms = tuple(jnp.full((1, TM), M_INIT, F32) for _ in heads)
    accs = tuple(jnp.zeros((V_PAD, TM), F32) for _ in heads)
    pts = tuple((jnp.ones((1, TM), F32), jnp.zeros((KC, TM), BF16)) for _ in heads)
    carry = (scores(0), ms, pts, accs)
    if n_chunks > 1:
        carry = lax.fori_loop(1, n_chunks, step, carry, unroll=n_chunks - 1)
    sts, ms, pts, accs = carry
    accs = accumulate(n_chunks - 2, pts, accs)
    _, pts = softmax(sts, ms)
    accs = accumulate(n_chunks - 1, pts, accs)
    outs = [acc[0:V_HEAD] / acc[V_HEAD:V_HEAD + 1] for acc in accs]
    o_ref[0, :, q0:q0 + TM] = jnp.concatenate(outs, axis=0).astype(BF16)


def _attention(qt, k, vt, q_tile_lo, n_q_tiles, first_key):
    b, t_all, _ = k.shape
    groups = MLA_HEADS // ATTN_HEADS
    per = ATTN_Q_TILES if n_q_tiles % ATTN_Q_TILES == 0 and q_tile_lo % ATTN_Q_TILES == 0 else 1
    tq = per * TM
    lo = q_tile_lo // per
    n_keys = t_all - first_key
    assert first_key % n_keys == 0
    key_blk = first_key // n_keys
    return pl.pallas_call(
        _attn_kernel,
        out_shape=jax.ShapeDtypeStruct((b, MLA_HEADS * V_HEAD, n_q_tiles * TM), BF16),
        grid=(b, groups, n_q_tiles // per),
        in_specs=[
            pl.BlockSpec((1, ATTN_HEADS * HEAD_PAD, tq), lambda bi, j, i: (bi, j, i + lo)),
            pl.BlockSpec((1, n_keys, ATTN_HEADS * HEAD_PAD), lambda bi, j, i: (bi, key_blk, j)),
            pl.BlockSpec((1, ATTN_HEADS * V_PAD, n_keys), lambda bi, j, i: (bi, j, key_blk)),
        ],
        out_specs=pl.BlockSpec((1, ATTN_HEADS * V_HEAD, tq), lambda bi, j, i: (bi, j, i)),
        compiler_params=pltpu.CompilerParams(
            dimension_semantics=("arbitrary", "arbitrary", "arbitrary"), vmem_limit_bytes=VMEM_LIMIT),
        name="attention",
    )(qt, k, vt)


def _fill_halo_buffer(buf, cur_ref, prev_ref, next_ref, keep_prev, keep_next):
    buf[0:HALO, :] = prev_ref[0] * keep_prev
    buf[HALO:HALO + TM, :] = cur_ref[0]
    buf[HALO + TM:HALO + TM + HALO, :] = next_ref[0] * keep_next


def _depthwise(buf, w_ref, width):
    base = HALO - (width - 1) // 2
    out = None
    for r in range(SUBLANES):
        acc = None
        for k in range(width):
            if (base + k) % SUBLANES != r:
                continue
            off = base + k - r
            term = w_ref[k:k + 1, :] * buf[off:off + TM + SUBLANES, :]
            acc = term if acc is None else acc + term
        if acc is not None:
            out = acc[r:r + TM, :] if out is None else out + acc[r:r + TM, :]
    return out


def _merge_kernel(*refs, n_lat_tiles, x_split, att_split):
    refs = list(refs)
    x_ref = refs.pop(0)
    xc_ref = refs.pop(0) if x_split else None
    att_ref = refs.pop(0)
    attc_ref = refs.pop(0) if att_split else None
    (mod_ref, glu_ref, glu_p_ref, glu_n_ref, cx_ref, cx_p_ref, cx_n_ref, bg_ref, lng_ref, wg_ref,
     wmo_ref, dw_ref, dwb_ref, cg_ref, cb_ref, wco_ref, scw_ref, wso_ref, wo_ref, o_ref, cbuf,
     sbuf) = refs
    i = pl.program_id(1)
    is_ctx = i >= n_lat_tiles
    keep_prev = jnp.where((i == 0) | is_ctx, 0.0, 1.0)
    keep_next = jnp.where((i == n_lat_tiles - 1) | is_ctx, 0.0, 1.0)
    _fill_halo_buffer(cbuf, glu_ref, glu_p_ref, glu_n_ref, keep_prev, keep_next)
    _fill_halo_buffer(sbuf, cx_ref, cx_p_ref, cx_n_ref, keep_prev, keep_next)

    x = _tile_rows(x_ref, xc_ref, n_lat_tiles)
    d = x.shape[-1]
    h = _norm_modulate(x, lng_ref[...], mod_ref[0:1, :], mod_ref[1:2, :]).astype(BF16)

    def gate(branch):
        return _sigmoid(_dot(h, wg_ref[:, branch * d:(branch + 1) * d]))

    att_t = _tile_rows(att_ref, attc_ref, n_lat_tiles)
    merged = gate(0) * lax.dot_general(att_t, wmo_ref[:, 0:d], (((0,), (0,)), ((), ())),
                                       preferred_element_type=F32)

    u = _depthwise(cbuf, dw_ref, CONV_WIDTH) + dwb_ref[...]
    mu = jnp.mean(u, axis=-1, keepdims=True)
    uc = u - mu
    var = jnp.mean(uc * uc, axis=-1, keepdims=True)
    u = uc * lax.rsqrt(var + EPS) * cg_ref[...] + cb_ref[...]
    u = u * _sigmoid(u)
    merged = merged + gate(1) * _dot(u.astype(BF16), wco_ref[:, 0:d])

    sc = bg_ref[0] * _depthwise(sbuf, scw_ref, SC_WIDTH)
    merged = merged + gate(2) * _dot(sc.astype(BF16), wso_ref[:, 0:d])

    o_ref[0] = x + mod_ref[2:3, :] * _dot(merged.astype(BF16), wo_ref[:, 0:d])


def _merge(l, xs, atts, mods, glu, cx, bg, lng, wg, wmo, dw, dwb, cg, cb, wco, scw, wso, wo, n_tiles,
           n_lat_tiles):
    b, _, d = xs[0].shape
    per = TM // HALO
    last_halo = glu.shape[1] // HALO - 1
    row = lambda bi, i: (bi, i, 0)
    prev = lambda bi, i: (bi, jnp.maximum(i * per - 1, 0), 0)
    nxt = lambda bi, i: (bi, jnp.minimum((i + 1) * per, last_halo), 0)
    params = (lng, wg, wmo, dw, dwb, cg, cb, wco, scw, wso, wo)
    return pl.pallas_call(
        functools.partial(_merge_kernel, n_lat_tiles=n_lat_tiles, x_split=len(xs) == 2,
                          att_split=len(atts) == 2),
        out_shape=jax.ShapeDtypeStruct((b, n_tiles * TM, d), F32),
        grid=(b, n_tiles),
        in_specs=(
            _stream_specs(xs, n_lat_tiles)
            + _stream_specs_t(atts, n_lat_tiles)
            + [_mod_spec(l, mods, n_lat_tiles, b),
               pl.BlockSpec((1, TM, CONV_CH), row),
               pl.BlockSpec((1, HALO, CONV_CH), prev),
               pl.BlockSpec((1, HALO, CONV_CH), nxt),
               pl.BlockSpec((1, TM, SC_CH), row),
               pl.BlockSpec((1, HALO, SC_CH), prev),
               pl.BlockSpec((1, HALO, SC_CH), nxt),
               pl.BlockSpec((1, TM, SC_CH), row)]
            + [_layer_spec(l, p) for p in params]),
        out_specs=pl.BlockSpec((1, TM, d), row),
        scratch_shapes=[
            pltpu.VMEM((TM + 2 * HALO, CONV_CH), F32),
            pltpu.VMEM((TM + 2 * HALO, SC_CH), F32),
        ],
        compiler_params=pltpu.CompilerParams(
            dimension_semantics=("arbitrary", "arbitrary"), vmem_limit_bytes=VMEM_LIMIT),
        name="merge",
    )(*xs, *atts, mods, glu, glu, glu, cx, cx, cx, bg, *params)


def _ffn_kernel(x_ref, mod_ref, modc_ref, lng_ref, w1_ref, w3_ref, w2_ref, fg_ref, o_ref, *,
                n_lat_tiles, final_norm):
    per = x_ref.shape[1] // TM
    for sub in range(per):
        rows = slice(sub * TM, (sub + 1) * TM)
        is_ctx = pl.program_id(1) * per + sub >= n_lat_tiles
        mod = jnp.where(is_ctx, modc_ref[...], mod_ref[...])
        x = x_ref[0, rows, :]
        h = _norm_modulate(x, lng_ref[...], mod[3:4, :], mod[4:5, :]).astype(BF16)
        a = _dot(h, w1_ref[...])
        a = a * _sigmoid(a) * _dot(h, w3_ref[...])
        y = x + mod[5:6, :] * _dot(a.astype(BF16), w2_ref[...])
        if final_norm:
            y = _rms(y, fg_ref[...])
        o_ref[0, rows, :] = y


def _ffn(l, xa, mods, lng, w1, w3, w2, fg, n_tiles, n_lat_tiles, final_norm):
    b, _, d = xa.shape
    per = max(p for p in range(1, FFN_TILES + 1) if n_tiles % p == 0)
    row = lambda bi, i: (bi, i, 0)
    n_mod = mods.shape[2]
    return pl.pallas_call(
        functools.partial(_ffn_kernel, n_lat_tiles=n_lat_tiles, final_norm=final_norm),
        out_shape=jax.ShapeDtypeStruct((b, n_tiles * TM, d), F32),
        grid=(b, n_tiles // per),
        in_specs=[
            pl.BlockSpec((1, per * TM, d), row),
            pl.BlockSpec((None, None, n_mod, d), lambda bi, i: (l, bi, 0, 0)),
            pl.BlockSpec((None, None, n_mod, d), lambda bi, i: (l, b, 0, 0)),
            _layer_spec(l, lng),
            _layer_spec(l, w1),
            _layer_spec(l, w3),
            _layer_spec(l, w2),
            pl.BlockSpec((1, d), lambda bi, i: (0, 0)),
        ],
        out_specs=pl.BlockSpec((1, per * TM, d), row),
        compiler_params=pltpu.CompilerParams(
            dimension_semantics=("arbitrary", "arbitrary"), vmem_limit_bytes=VMEM_LIMIT),
        name="ffn",
    )(xa, mods, mods, lng, w1, w3, w2, fg)


def _rope_patterns(n_lat, n_ctx):
    t = np.arange(n_lat)
    half = QK_ROPE // 2
    inv = np.float64(ROPE_THETA) ** (-np.arange(0, half, 2, dtype=np.float64) / half)
    ang_r = (t // GRID_W)[:, None] * inv
    ang_c = (t % GRID_W)[:, None] * inv
    cos = np.concatenate([np.cos(ang_r)] * 2 + [np.cos(ang_c)] * 2, axis=1)
    sin = np.concatenate([-np.sin(ang_r), np.sin(ang_r), -np.sin(ang_c), np.sin(ang_c)], axis=1)
    pad = HEAD_PAD - QK_NOPE - QK_ROPE
    rc = np.concatenate([np.ones((n_lat, QK_NOPE)), cos, np.zeros((n_lat, pad))], axis=1)
    rs = np.concatenate([np.zeros((n_lat, QK_NOPE)), sin, np.zeros((n_lat, pad))], axis=1)
    rc_ctx = np.concatenate([np.ones((n_ctx, QK_NOPE + QK_ROPE)), np.zeros((n_ctx, pad))], axis=1)
    rc = np.concatenate([rc, rc_ctx], axis=0).astype(np.float32)
    rs = np.concatenate([rs, np.zeros((n_ctx, HEAD_PAD))], axis=0).astype(np.float32)
    return rc, rs, np.ascontiguousarray(rc.T), np.ascontiguousarray(rs.T)


def _swap_rope_columns(w):
    idx = jnp.arange(QK_ROPE) ^ (QK_ROPE // 4)
    return w[..., idx]


def _projection_weights(w_in, w_q_b, w_kv_b):
    depth, d, _ = w_in.shape
    w_in, w_q_b, w_kv_b = (w.astype(BF16) for w in (w_in, w_q_b, w_kv_b))
    q_end = Q_LORA
    kv_end = q_end + KV_LORA + QK_ROPE
    sc_end = kv_end + 2 * CONV_CH + 3 * SC_CH
    w_kr = w_in[..., q_end + KV_LORA:kv_end]
    assert HEAD_PAD == QK_NOPE + 2 * QK_ROPE
    pad_lo = jnp.zeros((depth, d, QK_NOPE), w_in.dtype)
    wa = jnp.concatenate([
        w_in[..., :q_end + KV_LORA], w_in[..., kv_end:sc_end],
        pad_lo, w_kr, _swap_rope_columns(w_kr)], axis=-1).astype(BF16)
    wg = _pad_cols(w_in[..., sc_end:])

    wq = w_q_b.reshape(depth, Q_LORA, MLA_HEADS, QK_NOPE + QK_ROPE)
    wqm = jnp.concatenate([wq, _swap_rope_columns(wq[..., QK_NOPE:])], axis=-1)
    wqm = wqm.reshape(depth, Q_LORA, MLA_HEADS * HEAD_PAD)

    wkv = w_kv_b.reshape(depth, KV_LORA, MLA_HEADS, QK_NOPE + V_HEAD)
    wk = jnp.concatenate(
        [wkv[..., :QK_NOPE],
         jnp.zeros((depth, KV_LORA, MLA_HEADS, HEAD_PAD - QK_NOPE), w_kv_b.dtype)], axis=-1)
    wk = wk.reshape(depth, KV_LORA, MLA_HEADS * HEAD_PAD).astype(BF16)
    wv = jnp.concatenate(
        [wkv[..., QK_NOPE:],
         jnp.zeros((depth, KV_LORA, MLA_HEADS, V_PAD - V_HEAD), w_kv_b.dtype)], axis=-1)
    wvt = jnp.swapaxes(wv.reshape(depth, KV_LORA, MLA_HEADS * V_PAD), 1, 2).astype(BF16)
    return wa, wg, jnp.swapaxes(wqm, 1, 2).astype(BF16), wk, wvt


def kernel(x, c, ctx, c_ctx, w_mod, b_mod, ln1_g, w_in, q_a_norm_g, w_q_b, kv_a_norm_g, w_kv_b, w_mla_o, conv_dw, conv_dw_b, conv_ln_g, conv_ln_b, w_conv_o, sc_dw, w_sc_o, w_o, ln2_g, w_ff1, w_ff3, w_ff2, final_g):
    b, n_lat, d = x.shape
    n_ctx = ctx.shape[1]
    depth = w_mod.shape[0]
    assert n_ctx == TM and n_lat % TM == 0 and n_lat % KC == 0 and n_ctx % KC == 0
    assert b + 1 <= MOD_ROWS
    n_lat_tiles = n_lat // TM
    n_all_tiles = (n_lat + n_ctx) // TM

    act = jnp.concatenate([c, c_ctx[None, :], jnp.zeros((MOD_ROWS - b - 1, d), F32)], axis=0)
    mods = _modulation(act, w_mod, b_mod).reshape(depth, MOD_ROWS, N_MOD, d)

    rc, rs, rct, rst = _rope_patterns(n_lat, n_ctx)
    vec = lambda v: v.reshape(depth, 1, -1)
    wa, wg, wqmt, wk, wvt = _projection_weights(w_in, w_q_b, w_kv_b)
    wmo, wco, wso, wo = (_pad_cols(w) for w in (w_mla_o, w_conv_o, w_sc_o, w_o))
    w1, w3, w2 = (w.astype(BF16) for w in (w_ff1, w_ff3, w_ff2))

    xs = (x, ctx)
    for l in range(depth):
        last = l == depth - 1
        qt, k, vt, glu, cx, bg = _inproj(
            l, xs, mods, vec(ln1_g), wa, vec(q_a_norm_g), wqmt, vec(kv_a_norm_g), wk, wvt,
            rc, rs, rct, rst, n_all_tiles, n_lat_tiles)
        atts = (_attention(qt, k, vt, 0, n_lat_tiles, 0),)
        if not last:
            atts += (_attention(qt, k, vt, n_lat_tiles, n_all_tiles - n_lat_tiles, n_lat),)
        n_tiles = n_lat_tiles if last else n_all_tiles
        xa = _merge(l, xs, atts, mods, glu, cx, bg, vec(ln1_g), wg, wmo, conv_dw, vec(conv_dw_b),
                    vec(conv_ln_g), vec(conv_ln_b), wco, sc_dw, wso, wo, n_tiles, n_lat_tiles)
        xa = _ffn(l, xa, mods, vec(ln2_g), w1, w3, w2, final_g.reshape(1, -1), n_tiles,
                  n_lat_tiles, last)
        xs = (xa,)
    return xa
```

```python
import functools
import math

import jax
import jax.numpy as jnp
import numpy as np
from jax import lax
from jax.experimental import pallas as pl
from jax.experimental.pallas import tpu as pltpu

F32 = jnp.float32
BF16 = jnp.bfloat16

GRID_W = 64
MLA_HEADS = 8
QK_NOPE = 64
QK_ROPE = 32
V_HEAD = 64
Q_LORA = 384
KV_LORA = 256
CONV_CH = 256
CONV_WIDTH = 31
SC_CH = 256
SC_WIDTH = 3
N_MOD = 6
ROPE_THETA = 10000.0
EPS = 1e-6

LANES = 128
SUBLANES = 8
HEAD_PAD = LANES
TM = 256
HALO = 16
KC = 256
V_PAD = 80
M_INIT = -1e30
ATTN_HEADS = 2
ATTN_Q_TILES = 4
FFN_TILES = 3
MOD_ROWS = 8
MOD_TN = 1536
VMEM_LIMIT = 56 * 1024 * 1024

Q_SCALE = (QK_NOPE + QK_ROPE) ** -0.5 * math.log2(math.e)

A_Q = 0
A_KV = A_Q + Q_LORA
A_GLU_A = A_KV + KV_LORA
A_GLU_G = A_GLU_A + CONV_CH
A_SC_B = A_GLU_G + CONV_CH
A_SC_C = A_SC_B + SC_CH
A_SC_X = A_SC_C + SC_CH
A_KR = A_SC_X + SC_CH
A_COLS = A_KR + HEAD_PAD


def _dot(a, b):
    return jnp.dot(a, b, preferred_element_type=F32)


def _rms(x, g):
    return x * lax.rsqrt(jnp.mean(x * x, axis=-1, keepdims=True) + EPS) * g


def _norm_modulate(x, g, shift, scale):
    return _rms(x, g) * (1.0 + scale) + shift


def _sigmoid(x):
    return 0.5 * jnp.tanh(0.5 * x) + 0.5


def _pad_cols(w):
    return jnp.pad(w, ((0, 0),) * (w.ndim - 1) + ((0, LANES),)).astype(BF16)


def _mod_kernel(act_ref, w_ref, b_ref, o_ref):
    a = act_ref[...]
    a = a * _sigmoid(a)
    o_ref[0] = _dot(a.astype(BF16), w_ref[0].astype(BF16)) + b_ref[0]


def _modulation(act, w_mod, b_mod):
    depth, d, n = w_mod.shape
    return pl.pallas_call(
        _mod_kernel,
        out_shape=jax.ShapeDtypeStruct((depth, MOD_ROWS, n), F32),
        grid=(depth, n // MOD_TN),
        in_specs=[
            pl.BlockSpec((MOD_ROWS, d), lambda l, j: (0, 0)),
            pl.BlockSpec((1, d, MOD_TN), lambda l, j: (l, 0, j)),
            pl.BlockSpec((1, 1, MOD_TN), lambda l, j: (l, 0, j)),
        ],
        out_specs=pl.BlockSpec((1, MOD_ROWS, MOD_TN), lambda l, j: (l, 0, j)),
        compiler_params=pltpu.CompilerParams(
            dimension_semantics=("arbitrary", "arbitrary"), vmem_limit_bytes=VMEM_LIMIT),
        name="modulation",
    )(act, w_mod, b_mod.reshape(depth, 1, n))


def _dot_t(a, b):
    return lax.dot_general(a, b, (((1,), (1,)), ((), ())), preferred_element_type=F32)


def _tile_rows(lat_ref, ctx_ref, n_lat_tiles):
    if ctx_ref is None:
        return lat_ref[0]
    return jnp.where(pl.program_id(1) >= n_lat_tiles, ctx_ref[0], lat_ref[0])


def _inproj_kernel(*refs, n_lat_tiles, split):
    x_ref, ctx_ref = (refs[0], refs[1]) if split else (refs[0], None)
    (mod_ref, lng_ref, wa_ref, qg_ref, wqmt_ref, kvg_ref, wk_ref, wvt_ref, rc_ref, rs_ref,
     rct_ref, rst_ref, qt_ref, k_ref, vt_ref, glu_ref, cx_ref, bg_ref) = refs[2 if split else 1:]
    x = _tile_rows(x_ref, ctx_ref, n_lat_tiles)
    h = _norm_modulate(x, lng_ref[...], mod_ref[0:1, :], mod_ref[1:2, :])
    proj = _dot(h.astype(BF16), wa_ref[...])
    rope = slice(QK_NOPE, QK_NOPE + QK_ROPE)
    swap = slice(QK_NOPE + QK_ROPE, HEAD_PAD)

    qn = _rms(proj[:, A_Q:A_Q + Q_LORA], qg_ref[...] * Q_SCALE).astype(BF16)
    qm = _dot_t(wqmt_ref[...], qn)
    cos_t, sin_t = rct_ref[rope, :], rst_ref[rope, :]
    pieces = []
    for hh in range(MLA_HEADS):
        head = qm[hh * HEAD_PAD:(hh + 1) * HEAD_PAD]
        pieces += [head[0:QK_NOPE], head[rope] * cos_t + head[swap] * sin_t,
                   jnp.zeros((HEAD_PAD - QK_NOPE - QK_ROPE, TM), F32)]
    qt_ref[0] = jnp.concatenate(pieces, axis=0).astype(BF16)

    cn = _rms(proj[:, A_KV:A_KV + KV_LORA], kvg_ref[...]).astype(BF16)
    kr = proj[:, A_KR:A_KR + HEAD_PAD]
    kr = kr * rc_ref[...] + pltpu.roll(kr, HEAD_PAD - QK_ROPE, axis=1) * rs_ref[...]
    k_ref[0] = (_dot(cn, wk_ref[...]) + jnp.concatenate([kr] * MLA_HEADS, axis=1)).astype(BF16)

    vt = _dot_t(wvt_ref[...], cn)
    rowid = lax.broadcasted_iota(jnp.int32, vt.shape, 0)
    vt_ref[0] = jnp.where(rowid % V_PAD == V_HEAD, 1.0, vt).astype(BF16)

    glu_ref[0] = proj[:, A_GLU_A:A_GLU_A + CONV_CH] * _sigmoid(proj[:, A_GLU_G:A_GLU_G + CONV_CH])
    bg_ref[0] = proj[:, A_SC_B:A_SC_B + SC_CH]
    cx_ref[0] = proj[:, A_SC_C:A_SC_C + SC_CH] * proj[:, A_SC_X:A_SC_X + SC_CH]


def _stream_specs(arrs, n_lat_tiles):
    width = arrs[0].shape[-1]
    if len(arrs) == 1:
        return [pl.BlockSpec((1, TM, width), lambda bi, i: (bi, i, 0))]
    assert arrs[1].shape[1] == TM
    return [pl.BlockSpec((1, TM, width), lambda bi, i: (bi, jnp.minimum(i, n_lat_tiles - 1), 0)),
            pl.BlockSpec((1, TM, width), lambda bi, i: (bi, 0, 0))]


def _stream_specs_t(arrs, n_lat_tiles):
    feat = arrs[0].shape[1]
    if len(arrs) == 1:
        return [pl.BlockSpec((1, feat, TM), lambda bi, i: (bi, 0, i))]
    assert arrs[1].shape[2] == TM
    return [pl.BlockSpec((1, feat, TM), lambda bi, i: (bi, 0, jnp.minimum(i, n_lat_tiles - 1))),
            pl.BlockSpec((1, feat, TM), lambda bi, i: (bi, 0, 0))]


def _layer_spec(l, arr):
    zeros = (0,) * (arr.ndim - 1)
    return pl.BlockSpec((None,) + arr.shape[1:], lambda bi, i: (l,) + zeros)


def _mod_spec(l, mods, n_lat_tiles, ctx_row):
    _, _, n_mod, d = mods.shape
    return pl.BlockSpec((None, None, n_mod, d),
                        lambda bi, i: (l, jnp.where(i < n_lat_tiles, bi, ctx_row), 0, 0))


def _inproj(l, xs, mods, lng, wa, qg, wqmt, kvg, wk, wvt, rc, rs, rct, rst, n_tiles,
            n_lat_tiles):
    b, _, d = xs[0].shape
    t_all = n_tiles * TM
    hq = MLA_HEADS * HEAD_PAD
    hv = MLA_HEADS * V_PAD
    row = lambda bi, i: (bi, i, 0)
    col = lambda bi, i: (bi, 0, i)
    params = (lng, wa, qg, wqmt, kvg, wk, wvt)
    return pl.pallas_call(
        functools.partial(_inproj_kernel, n_lat_tiles=n_lat_tiles, split=len(xs) == 2),
        out_shape=(
            jax.ShapeDtypeStruct((b, hq, t_all), BF16),
            jax.ShapeDtypeStruct((b, t_all, hq), BF16),
            jax.ShapeDtypeStruct((b, hv, t_all), BF16),
            jax.ShapeDtypeStruct((b, t_all, CONV_CH), F32),
            jax.ShapeDtypeStruct((b, t_all, SC_CH), F32),
            jax.ShapeDtypeStruct((b, t_all, SC_CH), F32),
        ),
        grid=(b, n_tiles),
        in_specs=(
            _stream_specs(xs, n_lat_tiles)
            + [_mod_spec(l, mods, n_lat_tiles, b)]
            + [_layer_spec(l, p) for p in params]
            + [pl.BlockSpec((TM, HEAD_PAD), lambda bi, i: (i, 0)),
               pl.BlockSpec((TM, HEAD_PAD), lambda bi, i: (i, 0)),
               pl.BlockSpec((HEAD_PAD, TM), lambda bi, i: (0, i)),
               pl.BlockSpec((HEAD_PAD, TM), lambda bi, i: (0, i))]),
        out_specs=(
            pl.BlockSpec((1, hq, TM), col),
            pl.BlockSpec((1, TM, hq), row),
            pl.BlockSpec((1, hv, TM), col),
            pl.BlockSpec((1, TM, CONV_CH), row),
            pl.BlockSpec((1, TM, SC_CH), row),
            pl.BlockSpec((1, TM, SC_CH), row),
        ),
        compiler_params=pltpu.CompilerParams(
            dimension_semantics=("arbitrary", "arbitrary"), vmem_limit_bytes=VMEM_LIMIT),
        name="inproj",
    )(*xs, mods, *params, rc, rs, rct, rst)


def _attn_kernel(qt_ref, k_ref, vt_ref, o_ref):
    tiles = [_attn_tile(qt_ref, k_ref, vt_ref, o_ref, sub * TM)
             for sub in range(qt_ref.shape[2] // TM)]
    next(tiles[0])
    next(tiles[0])
    for t in range(1, len(tiles)):
        next(tiles[t])
        next(tiles[t - 1], None)
        next(tiles[t])
    next(tiles[-1], None)


def _attn_tile(qt_ref, k_ref, vt_ref, o_ref, q0):
    n_chunks = k_ref.shape[1] // KC
    heads = range(ATTN_HEADS)
    qts = [qt_ref[0, hh * HEAD_PAD:(hh + 1) * HEAD_PAD, q0:q0 + TM] for hh in heads]

    def scores(kb):
        start = pl.multiple_of(kb * KC, KC)
        return tuple(_dot(k_ref[0, pl.ds(start, KC), hh * HEAD_PAD:(hh + 1) * HEAD_PAD], qts[hh])
                     for hh in heads)

    def softmax(sts, ms):
        out = []
        for hh in heads:
            m_new = jnp.maximum(ms[hh], jnp.max(sts[hh], axis=0, keepdims=True))
            out.append((m_new, jnp.exp2(ms[hh] - m_new), jnp.exp2(sts[hh] - m_new).astype(BF16)))
        return tuple(o[0] for o in out), tuple(o[1:] for o in out)

    def accumulate(kb, pts, accs):
        start = pl.multiple_of(jnp.maximum(kb, 0) * KC, KC)
        return tuple(
            pts[hh][0] * accs[hh] + _dot(vt_ref[0, hh * V_PAD:(hh + 1) * V_PAD, pl.ds(start, KC)],
                                         pts[hh][1])
            for hh in heads)

    def step(kb, carry):
        sts, ms, pts, accs = carry
        ms_new, pts_new = softmax(sts, ms)
        return scores(kb), ms_new, pts_new, accumulate(kb - 2, pts, accs)

    ms = tuple(jnp.full((1, TM), M_INIT, F32) for _ in heads)
    accs = tuple(jnp.zeros((V_PAD, TM), F32) for _ in heads)
    pts = tuple((jnp.ones((1, TM), F32), jnp.zeros((KC, TM), BF16)) for _ in heads)
    carry = (scores(0), ms, pts, accs)
    yield
    if n_chunks > 1:
        carry = lax.fori_loop(1, n_chunks, step, carry, unroll=n_chunks - 1)
    yield
    sts, ms, pts, accs = carry
    accs = accumulate(n_chunks - 2, pts, accs)
    _, pts = softmax(sts, ms)
    accs = accumulate(n_chunks - 1, pts, accs)
    outs = [acc[0:V_HEAD] / acc[V_HEAD:V_HEAD + 1] for acc in accs]
    o_ref[0, :, q0:q0 + TM] = jnp.concatenate(outs, axis=0).astype(BF16)


def _attention(qt, k, vt, q_tile_lo, n_q_tiles, first_key):
    b, t_all, _ = k.shape
    groups = MLA_HEADS // ATTN_HEADS
    per = ATTN_Q_TILES if n_q_tiles % ATTN_Q_TILES == 0 and q_tile_lo % ATTN_Q_TILES == 0 else 1
    tq = per * TM
    lo = q_tile_lo // per
    n_keys = t_all - first_key
    assert first_key % n_keys == 0
    key_blk = first_key // n_keys
    return pl.pallas_call(
        _attn_kernel,
        out_shape=jax.ShapeDtypeStruct((b, MLA_HEADS * V_HEAD, n_q_tiles * TM), BF16),
        grid=(b, groups, n_q_tiles // per),
        in_specs=[
            pl.BlockSpec((1, ATTN_HEADS * HEAD_PAD, tq), lambda bi, j, i: (bi, j, i + lo)),
            pl.BlockSpec((1, n_keys, ATTN_HEADS * HEAD_PAD), lambda bi, j, i: (bi, key_blk, j)),
            pl.BlockSpec((1, ATTN_HEADS * V_PAD, n_keys), lambda bi, j, i: (bi, j, key_blk)),
        ],
        out_specs=pl.BlockSpec((1, ATTN_HEADS * V_HEAD, tq), lambda bi, j, i: (bi, j, i)),
        compiler_params=pltpu.CompilerParams(
            dimension_semantics=("arbitrary", "arbitrary", "arbitrary"), vmem_limit_bytes=VMEM_LIMIT),
        name="attention",
    )(qt, k, vt)


def _fill_halo_buffer(buf, cur_ref, prev_ref, next_ref, keep_prev, keep_next):
    buf[0:HALO, :] = prev_ref[0] * keep_prev
    buf[HALO:HALO + TM, :] = cur_ref[0]
    buf[HALO + TM:HALO + TM + HALO, :] = next_ref[0] * keep_next


def _depthwise(buf, w_ref, width):
    base = HALO - (width - 1) // 2
    out = None
    for r in range(SUBLANES):
        acc = None
        for k in range(width):
            if (base + k) % SUBLANES != r:
                continue
            off = base + k - r
            term = w_ref[k:k + 1, :] * buf[off:off + TM + SUBLANES, :]
            acc = term if acc is None else acc + term
        if acc is not None:
            out = acc[r:r + TM, :] if out is None else out + acc[r:r + TM, :]
    return out


def _merge_kernel(*refs, n_lat_tiles, x_split, att_split):
    refs = list(refs)
    x_ref = refs.pop(0)
    xc_ref = refs.pop(0) if x_split else None
    att_ref = refs.pop(0)
    attc_ref = refs.pop(0) if att_split else None
    (mod_ref, glu_ref, glu_p_ref, glu_n_ref, cx_ref, cx_p_ref, cx_n_ref, bg_ref, lng_ref, wg_ref,
     wmo_ref, dw_ref, dwb_ref, cg_ref, cb_ref, wco_ref, scw_ref, wso_ref, wo_ref, o_ref, cbuf,
     sbuf) = refs
    i = pl.program_id(1)
    is_ctx = i >= n_lat_tiles
    keep_prev = jnp.where((i == 0) | is_ctx, 0.0, 1.0)
    keep_next = jnp.where((i == n_lat_tiles - 1) | is_ctx, 0.0, 1.0)
    _fill_halo_buffer(cbuf, glu_ref, glu_p_ref, glu_n_ref, keep_prev, keep_next)
    _fill_halo_buffer(sbuf, cx_ref, cx_p_ref, cx_n_ref, keep_prev, keep_next)

    x = _tile_rows(x_ref, xc_ref, n_lat_tiles)
    d = x.shape[-1]
    h = _norm_modulate(x, lng_ref[...], mod_ref[0:1, :], mod_ref[1:2, :]).astype(BF16)

    def gate(branch):
        return _sigmoid(_dot(h, wg_ref[:, branch * d:(branch + 1) * d]))

    att_t = _tile_rows(att_ref, attc_ref, n_lat_tiles)
    merged = gate(0) * lax.dot_general(att_t, wmo_ref[:, 0:d], (((0,), (0,)), ((), ())),
                                       preferred_element_type=F32)

    u = _depthwise(cbuf, dw_ref, CONV_WIDTH) + dwb_ref[...]
    mu = jnp.mean(u, axis=-1, keepdims=True)
    uc = u - mu
    var = jnp.mean(uc * uc, axis=-1, keepdims=True)
    u = uc * lax.rsqrt(var + EPS) * cg_ref[...] + cb_ref[...]
    u = u * _sigmoid(u)
    merged = merged + gate(1) * _dot(u.astype(BF16), wco_ref[:, 0:d])

    sc = bg_ref[0] * _depthwise(sbuf, scw_ref, SC_WIDTH)
    merged = merged + gate(2) * _dot(sc.astype(BF16), wso_ref[:, 0:d])

    o_ref[0] = x + mod_ref[2:3, :] * _dot(merged.astype(BF16), wo_ref[:, 0:d])


def _merge(l, xs, atts, mods, glu, cx, bg, lng, wg, wmo, dw, dwb, cg, cb, wco, scw, wso, wo, n_tiles,
           n_lat_tiles):
    b, _, d = xs[0].shape
    per = TM // HALO
    last_halo = glu.shape[1] // HALO - 1
    row = lambda bi, i: (bi, i, 0)
    prev = lambda bi, i: (bi, jnp.maximum(i * per - 1, 0), 0)
    nxt = lambda bi, i: (bi, jnp.minimum((i + 1) * per, last_halo), 0)
    params = (lng, wg, wmo, dw, dwb, cg, cb, wco, scw, wso, wo)
    return pl.pallas_call(
        functools.partial(_merge_kernel, n_lat_tiles=n_lat_tiles, x_split=len(xs) == 2,
                          att_split=len(atts) == 2),
        out_shape=jax.ShapeDtypeStruct((b, n_tiles * TM, d), F32),
        grid=(b, n_tiles),
        in_specs=(
            _stream_specs(xs, n_lat_tiles)
            + _stream_specs_t(atts, n_lat_tiles)
            + [_mod_spec(l, mods, n_lat_tiles, b),
               pl.BlockSpec((1, TM, CONV_CH), row),
               pl.BlockSpec((1, HALO, CONV_CH), prev),
               pl.BlockSpec((1, HALO, CONV_CH), nxt),
               pl.BlockSpec((1, TM, SC_CH), row),
               pl.BlockSpec((1, HALO, SC_CH), prev),
               pl.BlockSpec((1, HALO, SC_CH), nxt),
               pl.BlockSpec((1, TM, SC_CH), row)]
            + [_layer_spec(l, p) for p in params]),
        out_specs=pl.BlockSpec((1, TM, d), row),
        scratch_shapes=[
            pltpu.VMEM((TM + 2 * HALO, CONV_CH), F32),
            pltpu.VMEM((TM + 2 * HALO, SC_CH), F32),
        ],
        compiler_params=pltpu.CompilerParams(
            dimension_semantics=("arbitrary", "arbitrary"), vmem_limit_bytes=VMEM_LIMIT),
        name="merge",
    )(*xs, *atts, mods, glu, glu, glu, cx, cx, cx, bg, *params)


def _ffn_kernel(x_ref, mod_ref, modc_ref, lng_ref, w1_ref, w3_ref, w2_ref, fg_ref, o_ref, *,
                n_lat_tiles, final_norm):
    per = x_ref.shape[1] // TM
    for sub in range(per):
        rows = slice(sub * TM, (sub + 1) * TM)
        is_ctx = pl.program_id(1) * per + sub >= n_lat_tiles
        mod = jnp.where(is_ctx, modc_ref[...], mod_ref[...])
        x = x_ref[0, rows, :]
        h = _norm_modulate(x, lng_ref[...], mod[3:4, :], mod[4:5, :]).astype(BF16)
        a = _dot(h, w1_ref[...])
        a = a * _sigmoid(a) * _dot(h, w3_ref[...])
        y = x + mod[5:6, :] * _dot(a.astype(BF16), w2_ref[...])
        if final_norm:
            y = _rms(y, fg_ref[...])
        o_ref[0, rows, :] = y


def _ffn(l, xa, mods, lng, w1, w3, w2, fg, n_tiles, n_lat_tiles, final_norm):
    b, _, d = xa.shape
    per = max(p for p in range(1, FFN_TILES + 1) if n_tiles % p == 0)
    row = lambda bi, i: (bi, i, 0)
    n_mod = mods.shape[2]
    return pl.pallas_call(
        functools.partial(_ffn_kernel, n_lat_tiles=n_lat_tiles, final_norm=final_norm),
        out_shape=jax.ShapeDtypeStruct((b, n_tiles * TM, d), F32),
        grid=(b, n_tiles // per),
        in_specs=[
            pl.BlockSpec((1, per * TM, d), row),
            pl.BlockSpec((None, None, n_mod, d), lambda bi, i: (l, bi, 0, 0)),
            pl.BlockSpec((None, None, n_mod, d), lambda bi, i: (l, b, 0, 0)),
            _layer_spec(l, lng),
            _layer_spec(l, w1),
            _layer_spec(l, w3),
            _layer_spec(l, w2),
            pl.BlockSpec((1, d), lambda bi, i: (0, 0)),
        ],
        out_specs=pl.BlockSpec((1, per * TM, d), row),
        compiler_params=pltpu.CompilerParams(
            dimension_semantics=("arbitrary", "arbitrary"), vmem_limit_bytes=VMEM_LIMIT),
        name="ffn",
    )(xa, mods, mods, lng, w1, w3, w2, fg)


def _rope_patterns(n_lat, n_ctx):
    t = np.arange(n_lat)
    half = QK_ROPE // 2
    inv = np.float64(ROPE_THETA) ** (-np.arange(0, half, 2, dtype=np.float64) / half)
    ang_r = (t // GRID_W)[:, None] * inv
    ang_c = (t % GRID_W)[:, None] * inv
    cos = np.concatenate([np.cos(ang_r)] * 2 + [np.cos(ang_c)] * 2, axis=1)
    sin = np.concatenate([-np.sin(ang_r), np.sin(ang_r), -np.sin(ang_c), np.sin(ang_c)], axis=1)
    pad = HEAD_PAD - QK_NOPE - QK_ROPE
    rc = np.concatenate([np.ones((n_lat, QK_NOPE)), cos, np.zeros((n_lat, pad))], axis=1)
    rs = np.concatenate([np.zeros((n_lat, QK_NOPE)), sin, np.zeros((n_lat, pad))], axis=1)
    rc_ctx = np.concatenate([np.ones((n_ctx, QK_NOPE + QK_ROPE)), np.zeros((n_ctx, pad))], axis=1)
    rc = np.concatenate([rc, rc_ctx], axis=0).astype(np.float32)
    rs = np.concatenate([rs, np.zeros((n_ctx, HEAD_PAD))], axis=0).astype(np.float32)
    return rc, rs, np.ascontiguousarray(rc.T), np.ascontiguousarray(rs.T)


def _swap_rope_columns(w):
    idx = jnp.arange(QK_ROPE) ^ (QK_ROPE // 4)
    return w[..., idx]


def _projection_weights(w_in, w_q_b, w_kv_b):
    depth, d, _ = w_in.shape
    w_in, w_q_b, w_kv_b = (w.astype(BF16) for w in (w_in, w_q_b, w_kv_b))
    q_end = Q_LORA
    kv_end = q_end + KV_LORA + QK_ROPE
    sc_end = kv_end + 2 * CONV_CH + 3 * SC_CH
    w_kr = w_in[..., q_end + KV_LORA:kv_end]
    assert HEAD_PAD == QK_NOPE + 2 * QK_ROPE
    pad_lo = jnp.zeros((depth, d, QK_NOPE), w_in.dtype)
    wa = jnp.concatenate([
        w_in[..., :q_end + KV_LORA], w_in[..., kv_end:sc_end],
        pad_lo, w_kr, _swap_rope_columns(w_kr)], axis=-1).astype(BF16)
    wg = _pad_cols(w_in[..., sc_end:])

    wq = w_q_b.reshape(depth, Q_LORA, MLA_HEADS, QK_NOPE + QK_ROPE)
    wqm = jnp.concatenate([wq, _swap_rope_columns(wq[..., QK_NOPE:])], axis=-1)
    wqm = wqm.reshape(depth, Q_LORA, MLA_HEADS * HEAD_PAD)

    wkv = w_kv_b.reshape(depth, KV_LORA, MLA_HEADS, QK_NOPE + V_HEAD)
    wk = jnp.concatenate(
        [wkv[..., :QK_NOPE],
         jnp.zeros((depth, KV_LORA, MLA_HEADS, HEAD_PAD - QK_NOPE), w_kv_b.dtype)], axis=-1)
    wk = wk.reshape(depth, KV_LORA, MLA_HEADS * HEAD_PAD).astype(BF16)
    wv = jnp.concatenate(
        [wkv[..., QK_NOPE:],
         jnp.zeros((depth, KV_LORA, MLA_HEADS, V_PAD - V_HEAD), w_kv_b.dtype)], axis=-1)
    wvt = jnp.swapaxes(wv.reshape(depth, KV_LORA, MLA_HEADS * V_PAD), 1, 2).astype(BF16)
    return wa, wg, jnp.swapaxes(wqm, 1, 2).astype(BF16), wk, wvt


def kernel(x, c, ctx, c_ctx, w_mod, b_mod, ln1_g, w_in, q_a_norm_g, w_q_b, kv_a_norm_g, w_kv_b, w_mla_o, conv_dw, conv_dw_b, conv_ln_g, conv_ln_b, w_conv_o, sc_dw, w_sc_o, w_o, ln2_g, w_ff1, w_ff3, w_ff2, final_g):
    b, n_lat, d = x.shape
    n_ctx = ctx.shape[1]
    depth = w_mod.shape[0]
    assert n_ctx == TM and n_lat % TM == 0 and n_lat % KC == 0 and n_ctx % KC == 0
    assert b + 1 <= MOD_ROWS
    n_lat_tiles = n_lat // TM
    n_all_tiles = (n_lat + n_ctx) // TM

    act = jnp.concatenate([c, c_ctx[None, :], jnp.zeros((MOD_ROWS - b - 1, d), F32)], axis=0)
    mods = _modulation(act, w_mod, b_mod).reshape(depth, MOD_ROWS, N_MOD, d)

    rc, rs, rct, rst = _rope_patterns(n_lat, n_ctx)
    vec = lambda v: v.reshape(depth, 1, -1)
    wa, wg, wqmt, wk, wvt = _projection_weights(w_in, w_q_b, w_kv_b)
    wmo, wco, wso, wo = (_pad_cols(w) for w in (w_mla_o, w_conv_o, w_sc_o, w_o))
    w1, w3, w2 = (w.astype(BF16) for w in (w_ff1, w_ff3, w_ff2))

    xs = (x, ctx)
    for l in range(depth):
        last = l == depth - 1
        qt, k, vt, glu, cx, bg = _inproj(
            l, xs, mods, vec(ln1_g), wa, vec(q_a_norm_g), wqmt, vec(kv_a_norm_g), wk, wvt,
            rc, rs, rct, rst, n_all_tiles, n_lat_tiles)
        atts = (_attention(qt, k, vt, 0, n_lat_tiles, 0),)
        if not last:
            atts += (_attention(qt, k, vt, n_lat_tiles, n_all_tiles - n_lat_tiles, n_lat),)
        n_tiles = n_lat_tiles if last else n_all_tiles
        xa = _merge(l, xs, atts, mods, glu, cx, bg, vec(ln1_g), wg, wmo, conv_dw, vec(conv_dw_b),
                    vec(conv_ln_g), vec(conv_ln_b), wco, sc_dw, wso, wo, n_tiles, n_lat_tiles)
        xa = _ffn(l, xa, mods, vec(ln2_g), w1, w3, w2, final_g.reshape(1, -1), n_tiles,
                  n_lat_tiles, last)
        xs = (xa,)
    return xa
```
